```python
import jax
import jax.numpy as jnp
from jax import lax
import numpy as np

D_MODEL = 1024
BATCH = 8
SEQ = 8192
DEPTH = 2

GRID_W = 64
CTX_LEN = 256
CHUNK = 128
NORM_EPS = 1e-6
ROPE_BASE = 10000.0

GLA_HEADS = 4
GLA_K = D_MODEL // 2
GLA_V = D_MODEL
GLA_DK = GLA_K // GLA_HEADS
GLA_DV = GLA_V // GLA_HEADS
GLA_LOW_RANK = 16
GLA_TAU = 16.0

ATT_HEAD_DIM = 128
ATT_Q_HEADS = D_MODEL // ATT_HEAD_DIM
ATT_KV_HEADS = 2
ATT_WINDOW = 128
ATT_QBLOCK = 128

RET_HEADS = 4
RET_K = D_MODEL // 2
RET_V = D_MODEL
RET_DK = RET_K // RET_HEADS
RET_DV = RET_V // RET_HEADS

MOE_GROUPS = 4
MOE_EXPERTS_PER_GROUP = 8
MOE_EXPERTS = MOE_GROUPS * MOE_EXPERTS_PER_GROUP
MOE_TOP_K = 2
MOE_HIDDEN = D_MODEL
MOE_BLOCK = 256

IN_LAYOUT = (
    ('gla_q', GLA_K), ('gla_k', GLA_K), ('gla_v', GLA_V), ('gla_r', GLA_V), ('gla_lr', 2 * GLA_LOW_RANK),
    ('att_q', ATT_Q_HEADS * ATT_HEAD_DIM), ('att_k', ATT_KV_HEADS * ATT_HEAD_DIM), ('att_v', ATT_KV_HEADS * ATT_HEAD_DIM),
    ('ret_q', RET_K), ('ret_k', RET_K), ('ret_v', RET_V), ('ret_g', RET_V),
    ('gates', 3 * D_MODEL),
)
N_IN = (2 * GLA_K + 2 * GLA_V + 2 * GLA_LOW_RANK + ATT_Q_HEADS * ATT_HEAD_DIM + 2 * ATT_KV_HEADS * ATT_HEAD_DIM
        + 2 * RET_K + 2 * RET_V + 3 * D_MODEL)

kernel_name = 'hybrid_gla_swa_retention_hmoe_dit'


def rms_norm(x, g):
    x32 = x.astype(jnp.float32)
    y = x32 * lax.rsqrt(jnp.mean(x32 * x32, axis=-1, keepdims=True) + NORM_EPS)
    return y.astype(x.dtype) * g


def head_group_norm(y, g):
    b, t, h, dv = y.shape
    y32 = y.astype(jnp.float32)
    mu = jnp.mean(y32, axis=-1, keepdims=True)
    var = jnp.mean(jnp.square(y32 - mu), axis=-1, keepdims=True)
    yn = ((y32 - mu) * lax.rsqrt(var + NORM_EPS)).astype(y.dtype)
    return yn.reshape(b, t, h * dv) * g


def rope(x, pos):
    half = x.shape[-1] // 2
    inv_freq = ROPE_BASE ** (-jnp.arange(half, dtype=jnp.float32) / half)
    ang = pos.astype(jnp.float32)[:, None] * inv_freq[None, :]
    cos = jnp.cos(ang)[:, None, :].astype(x.dtype)
    sin = jnp.sin(ang)[:, None, :].astype(x.dtype)
    x1, x2 = x[..., :half], x[..., half:]
    return jnp.concatenate([x1 * cos - x2 * sin, x2 * cos + x1 * sin], axis=-1)


def axial_rope(x, rows, cols):
    h = x.shape[-1] // 2
    return jnp.concatenate([rope(x[..., :h], rows), rope(x[..., h:], cols)], axis=-1)


def chunk_recurrence(q, k, v, log_a, s0, strict):
    b_, t_, h_, dk = q.shape
    dv = v.shape[-1]
    n = t_ // CHUNK
    f32 = jnp.float32
    qc = q.astype(f32).reshape(b_, n, CHUNK, h_, dk)
    kc = k.astype(f32).reshape(b_, n, CHUNK, h_, dk)
    vc = v.astype(f32).reshape(b_, n, CHUNK, h_, dv)
    cum = jnp.cumsum(log_a.astype(f32).reshape(b_, n, CHUNK, h_, dk), axis=2)
    cum_last = cum[:, :, -1:]
    q_in = qc * jnp.exp(cum)
    k_in = kc * jnp.exp(-cum)
    k_out = kc * jnp.exp(cum_last - cum)
    idx = jnp.arange(CHUNK)
    mask = (idx[:, None] > idx[None, :]) if strict else (idx[:, None] >= idx[None, :])
    scores = jnp.where(mask, jnp.einsum('bnihd,bnjhd->bnhij', q_in, k_in), 0.0)
    o_intra = jnp.einsum('bnhij,bnjhe->bnihe', scores, vc)

    def step(s, xs):
        q_i, k_o, v_i, dec = xs
        o = jnp.einsum('bihd,bhde->bihe', q_i, s)
        s = s * dec[..., None] + jnp.einsum('bjhd,bjhe->bhde', k_o, v_i)
        return s, o

    xs = (jnp.moveaxis(q_in, 1, 0), jnp.moveaxis(k_out, 1, 0), jnp.moveaxis(vc, 1, 0),
          jnp.moveaxis(jnp.exp(cum_last[:, :, 0]), 1, 0))
    s_fin, o_inter = lax.scan(step, s0.astype(f32), xs)
    o = o_intra + jnp.moveaxis(o_inter, 0, 1)
    return o.reshape(b_, t_, h_, dv).astype(v.dtype), s_fin


def bidirectional_scan(q, k, v, la_fwd, la_bwd, n_ctx):
    b_, _, h_, dk = q.shape
    dv = v.shape[-1]
    la_fwd = jnp.broadcast_to(la_fwd, q.shape)
    la_bwd = jnp.broadcast_to(la_bwd, q.shape)
    s0 = jnp.zeros((b_, h_, dk, dv), jnp.float32)
    rev = lambda a: jnp.flip(a, axis=1)
    oc_f, sc_f = chunk_recurrence(q[:, :n_ctx], k[:, :n_ctx], v[:, :n_ctx], la_fwd[:, :n_ctx], s0, False)
    ol_f, _ = chunk_recurrence(q[:, n_ctx:], k[:, n_ctx:], v[:, n_ctx:], la_fwd[:, n_ctx:], sc_f, False)
    oc_b, sc_b = chunk_recurrence(rev(q[:, :n_ctx]), rev(k[:, :n_ctx]), rev(v[:, :n_ctx]), rev(la_bwd[:, :n_ctx]), s0, True)
    ol_b, _ = chunk_recurrence(rev(q[:, n_ctx:]), rev(k[:, n_ctx:]), rev(v[:, n_ctx:]), rev(la_bwd[:, n_ctx:]), sc_b, True)
    return jnp.concatenate([oc_f + rev(oc_b), ol_f + rev(ol_b)], axis=1)


def window_attention(q_lat, k_lat, v_lat, q_ctx, k_ctx, v_ctx, sink, with_ctx_queries):
    b_, l_, hq, hd = q_lat.shape
    g_ = hq // ATT_KV_HEADS
    c_ = k_ctx.shape[1]
    nb = l_ // ATT_QBLOCK
    span = ATT_QBLOCK + 2 * ATT_WINDOW
    scale = hd ** -0.5
    pad = ((0, 0), (ATT_WINDOW, ATT_WINDOW), (0, 0), (0, 0))
    k_pad = jnp.pad(k_lat, pad)
    v_pad = jnp.pad(v_lat, pad)
    sink_logit = sink.astype(jnp.float32).reshape(1, ATT_KV_HEADS, g_, 1, 1)
    q_blocks = jnp.moveaxis(q_lat.reshape(b_, nb, ATT_QBLOCK, ATT_KV_HEADS, g_, hd), 1, 0)

    def attend_block(args):
        qb, n = args
        start = n * ATT_QBLOCK
        kw = lax.dynamic_slice_in_dim(k_pad, start, span, axis=1)
        vw = lax.dynamic_slice_in_dim(v_pad, start, span, axis=1)
        q_pos = start + jnp.arange(ATT_QBLOCK)
        k_pos = start - ATT_WINDOW + jnp.arange(span)
        valid = ((jnp.abs(q_pos[:, None] - k_pos[None, :]) <= ATT_WINDOW)
                 & (k_pos >= 0)[None, :] & (k_pos < l_)[None, :])
        s_loc = jnp.where(valid, jnp.einsum('bqhgd,bkhd->bhgqk', qb, kw).astype(jnp.float32) * scale, -jnp.inf)
        s_ctx = jnp.einsum('bqhgd,bkhd->bhgqk', qb, k_ctx).astype(jnp.float32) * scale
        s_sink = jnp.broadcast_to(sink_logit, s_ctx.shape[:-1] + (1,))
        p = jax.nn.softmax(jnp.concatenate([s_sink, s_ctx, s_loc], axis=-1), axis=-1).astype(qb.dtype)
        return (jnp.einsum('bhgqk,bkhd->bqhgd', p[..., 1:1 + c_], v_ctx)
                + jnp.einsum('bhgqk,bkhd->bqhgd', p[..., 1 + c_:], vw))

    o_lat = lax.map(attend_block, (q_blocks, jnp.arange(nb)))
    o_lat = jnp.moveaxis(o_lat, 0, 1).reshape(b_, l_, hq * hd)
    if not with_ctx_queries:
        return o_lat
    qc = q_ctx.reshape(b_, c_, ATT_KV_HEADS, g_, hd)
    s = jnp.einsum('bqhgd,bkhd->bhgqk', qc, k_ctx).astype(jnp.float32) * scale
    s_sink = jnp.broadcast_to(sink_logit, s.shape[:-1] + (1,))
    p = jax.nn.softmax(jnp.concatenate([s_sink, s], axis=-1), axis=-1)[..., 1:].astype(q_ctx.dtype)
    o_ctx = jnp.einsum('bhgqk,bkhd->bqhgd', p, v_ctx).reshape(b_, c_, hq * hd)
    return jnp.concatenate([o_ctx, o_lat], axis=1)


def token_mixers(h, n_ctx, rows, cols, ret_pos, ret_log_decay, w_in, gla_wa2, gla_ba, gla_norm_g, attn_sink,
                 ret_norm_g, w_br_gla, w_br_attn, w_br_ret, w_out, latent_only):
    t0 = n_ctx if latent_only else 0
    offsets = {}
    start = 0
    for name, width in IN_LAYOUT:
        offsets[name] = (start, start + width)
        start += width

    def proj(name, hh):
        lo, hi = offsets[name]
        return hh @ w_in[:, lo:hi]

    def heads(a, n):
        return a.reshape(a.shape[0], a.shape[1], n, -1)

    h_out = h[:, t0:]

    lr_f, lr_b = jnp.split(proj('gla_lr', h), 2, axis=-1)
    la_f = jax.nn.log_sigmoid((lr_f @ gla_wa2[0] + gla_ba[0]).astype(jnp.float32)) / GLA_TAU
    la_b = jax.nn.log_sigmoid((lr_b @ gla_wa2[1] + gla_ba[1]).astype(jnp.float32)) / GLA_TAU
    o_gla = bidirectional_scan(heads(proj('gla_q', h), GLA_HEADS) * GLA_DK ** -0.5, heads(proj('gla_k', h), GLA_HEADS),
                               heads(proj('gla_v', h), GLA_HEADS), heads(la_f, GLA_HEADS), heads(la_b, GLA_HEADS), n_ctx)
    y_gla = head_group_norm(o_gla[:, t0:], gla_norm_g) * jax.nn.silu(proj('gla_r', h_out))

    aq = heads(proj('att_q', h), ATT_Q_HEADS)
    ak = heads(proj('att_k', h), ATT_KV_HEADS)
    av = heads(proj('att_v', h), ATT_KV_HEADS)
    y_att = window_attention(axial_rope(aq[:, n_ctx:], rows, cols), axial_rope(ak[:, n_ctx:], rows, cols), av[:, n_ctx:],
                             aq[:, :n_ctx], ak[:, :n_ctx], av[:, :n_ctx], attn_sink, not latent_only)

    rq = rope(heads(proj('ret_q', h), RET_HEADS), ret_pos)
    rk = rope(heads(proj('ret_k', h), RET_HEADS), ret_pos) * RET_DK ** -0.5
    o_ret = bidirectional_scan(rq, rk, heads(proj('ret_v', h), RET_HEADS), ret_log_decay, ret_log_decay, n_ctx)
    y_ret = head_group_norm(o_ret[:, t0:], ret_norm_g) * jax.nn.silu(proj('ret_g', h_out))

    g_gla, g_att, g_ret = jnp.split(jax.nn.sigmoid(proj('gates', h_out)), 3, axis=-1)
    merged = g_gla * (y_gla @ w_br_gla) + g_att * (y_att @ w_br_attn) + g_ret * (y_ret @ w_br_ret)
    return merged @ w_out


def hier_moe(h, w_grp, b_grp, w_exp, b_exp, w1, w3, w2):
    n_tok, d = h.shape
    grp_logits = (h @ w_grp + b_grp).astype(jnp.float32)
    grp = jnp.argmax(grp_logits, axis=-1)
    grp_prob = jnp.take_along_axis(jax.nn.softmax(grp_logits, axis=-1), grp[:, None], axis=-1)
    exp_logits = (h @ w_exp + b_exp).astype(jnp.float32).reshape(n_tok, MOE_GROUPS, MOE_EXPERTS_PER_GROUP)
    in_grp = jnp.take_along_axis(exp_logits, grp[:, None, None], axis=1)[:, 0]
    top_val, top_idx = lax.top_k(in_grp, MOE_TOP_K)
    gate = grp_prob * jax.nn.softmax(top_val, axis=-1)
    expert = (grp[:, None] * MOE_EXPERTS_PER_GROUP + top_idx).astype(jnp.int32)

    n_assign = n_tok * MOE_TOP_K
    e_flat = expert.reshape(n_assign)
    t_flat = jnp.repeat(jnp.arange(n_tok, dtype=jnp.int32), MOE_TOP_K)
    order = jnp.argsort(e_flat)
    e_s, t_s, w_s = e_flat[order], t_flat[order], gate.reshape(n_assign)[order]
    counts = jnp.zeros((MOE_EXPERTS,), jnp.int32).at[e_s].add(1)
    starts = jnp.cumsum(counts) - counts
    padded = (counts + MOE_BLOCK - 1) // MOE_BLOCK * MOE_BLOCK
    pad_end = jnp.cumsum(padded)
    pad_start = pad_end - padded
    dest = pad_start[e_s] + jnp.arange(n_assign, dtype=jnp.int32) - starts[e_s]
    n_blocks = (n_assign + MOE_EXPERTS * (MOE_BLOCK - 1) + MOE_BLOCK - 1) // MOE_BLOCK
    n_slots = n_blocks * MOE_BLOCK
    slot_tok = jnp.full((n_slots,), n_tok, jnp.int32).at[dest].set(t_s)
    slot_w = jnp.zeros((n_slots,), h.dtype).at[dest].set(w_s.astype(h.dtype))
    blk_exp = jnp.minimum(jnp.searchsorted(pad_end, jnp.arange(n_blocks, dtype=jnp.int32) * MOE_BLOCK, side='right'),
                          MOE_EXPERTS - 1)
    h_pad = jnp.concatenate([h, jnp.zeros((1, d), h.dtype)], axis=0)

    def step(acc, xs):
        tok, wgt, e = xs
        xb = h_pad[tok]
        yb = (jax.nn.silu(xb @ w1[e]) * (xb @ w3[e])) @ w2[e]
        return acc.at[tok].add(yb * wgt[:, None]), None

    acc, _ = lax.scan(step, jnp.zeros((n_tok + 1, d), h.dtype),
                      (slot_tok.reshape(n_blocks, MOE_BLOCK), slot_w.reshape(n_blocks, MOE_BLOCK), blk_exp))
    return acc[:n_tok]


def setup_inputs(seed: int = 0) -> dict:
    key = jax.random.key(seed)
    keys = iter(jax.random.split(key, 32))
    f32 = jnp.float32
    D = D_MODEL

    def nrm(shape, scale):
        return jax.random.normal(next(keys), shape, f32) * scale

    return {
        'x': nrm((BATCH, SEQ, D), 1.0),
        'c': nrm((BATCH, D), 1.0),
        'ctx': nrm((BATCH, CTX_LEN, D), 1.0),
        'c_ctx': nrm((D,), 1.0),
        'w_ada': nrm((DEPTH, D, 6 * D), 0.5 * D ** -0.5),
        'b_ada': nrm((DEPTH, 6 * D), 0.02),
        'norm1_g': 1.0 + nrm((DEPTH, D), 0.05),
        'norm2_g': 1.0 + nrm((DEPTH, D), 0.05),
        'w_in': nrm((DEPTH, D, N_IN), D ** -0.5),
        'gla_wa2': nrm((DEPTH, 2, GLA_LOW_RANK, GLA_K), GLA_LOW_RANK ** -0.5),
        'gla_ba': nrm((DEPTH, 2, GLA_K), 0.1),
        'gla_norm_g': 1.0 + nrm((DEPTH, GLA_V), 0.05),
        'attn_sink': nrm((DEPTH, ATT_Q_HEADS), 0.5),
        'ret_norm_g': 1.0 + nrm((DEPTH, RET_V), 0.05),
        'w_br_gla': nrm((DEPTH, GLA_V, D), GLA_V ** -0.5),
        'w_br_attn': nrm((DEPTH, ATT_Q_HEADS * ATT_HEAD_DIM, D), (ATT_Q_HEADS * ATT_HEAD_DIM) ** -0.5),
        'w_br_ret': nrm((DEPTH, RET_V, D), RET_V ** -0.5),
        'w_out': nrm((DEPTH, D, D), D ** -0.5),
        'moe_w_grp': nrm((DEPTH, D, MOE_GROUPS), D ** -0.5),
        'moe_b_grp': nrm((DEPTH, MOE_GROUPS), 0.01),
        'moe_w_exp': nrm((DEPTH, D, MOE_EXPERTS), D ** -0.5),
        'moe_b_exp': nrm((DEPTH, MOE_EXPERTS), 0.01),
        'moe_w1': nrm((DEPTH, MOE_EXPERTS, D, MOE_HIDDEN), D ** -0.5),
        'moe_w3': nrm((DEPTH, MOE_EXPERTS, D, MOE_HIDDEN), D ** -0.5),
        'moe_w2': nrm((DEPTH, MOE_EXPERTS, MOE_HIDDEN, D), MOE_HIDDEN ** -0.5),
        'final_g': 1.0 + nrm((D,), 0.05),
    }


def reference(x, c, ctx, c_ctx, w_ada, b_ada, norm1_g, norm2_g, w_in, gla_wa2, gla_ba, gla_norm_g, attn_sink,
              ret_norm_g, w_br_gla, w_br_attn, w_br_ret, w_out, moe_w_grp, moe_b_grp, moe_w_exp, moe_b_exp,
              moe_w1, moe_w3, moe_w2, final_g):
    b_, l_, d = x.shape
    n_ctx = ctx.shape[1]
    ROWS = l_ // GRID_W
    rows = jnp.broadcast_to(jnp.arange(ROWS)[:, None], (ROWS, GRID_W)).reshape(l_)
    cols = jnp.broadcast_to(jnp.arange(GRID_W)[None, :], (ROWS, GRID_W)).reshape(l_)
    ret_pos = jnp.arange(n_ctx + l_)
    ret_log_decay = jnp.log(1.0 - jnp.exp2(-5.0 - jnp.arange(RET_HEADS, dtype=jnp.float32))).reshape(1, 1, RET_HEADS, 1)
    silu_c = jax.nn.silu(c)
    silu_cc = jax.nn.silu(c_ctx)

    for layer in range(DEPTH):
        last = layer == DEPTH - 1
        mod = silu_c @ w_ada[layer] + b_ada[layer]
        mod_c = silu_cc @ w_ada[layer] + b_ada[layer]
        sh1, sc1, g1, sh2, sc2, g2 = jnp.split(mod[:, None, :], 6, axis=-1)
        sh1c, sc1c, g1c, sh2c, sc2c, g2c = jnp.split(mod_c, 6, axis=-1)

        h = jnp.concatenate([rms_norm(ctx, norm1_g[layer]) * (1.0 + sc1c) + sh1c,
                             rms_norm(x, norm1_g[layer]) * (1.0 + sc1) + sh1], axis=1)
        mix = token_mixers(h, n_ctx, rows, cols, ret_pos, ret_log_decay, w_in[layer], gla_wa2[layer], gla_ba[layer],
                           gla_norm_g[layer], attn_sink[layer], ret_norm_g[layer], w_br_gla[layer], w_br_attn[layer],
                           w_br_ret[layer], w_out[layer], last)
        x = x + g1 * mix[:, -l_:]
        if not last:
            ctx = ctx + g1c * mix[:, :n_ctx]

        h2 = rms_norm(x, norm2_g[layer]) * (1.0 + sc2) + sh2
        if not last:
            h2 = jnp.concatenate([rms_norm(ctx, norm2_g[layer]) * (1.0 + sc2c) + sh2c, h2], axis=1)
        f = hier_moe(h2.reshape(-1, d), moe_w_grp[layer], moe_b_grp[layer], moe_w_exp[layer], moe_b_exp[layer],
                     moe_w1[layer], moe_w3[layer], moe_w2[layer]).reshape(h2.shape)
        x = x + g2 * f[:, -l_:]
        if not last:
            ctx = ctx + g2c * f[:, :n_ctx]

    return rms_norm(x, final_g)
```

```python
import functools
import math

import jax
import jax.numpy as jnp
from jax import lax
from jax.experimental import pallas as pl
from jax.experimental.pallas import tpu as pltpu

F32 = jnp.float32
BF16 = jnp.bfloat16

D_MODEL = 1024
GRID_W = 64
CHUNK = 128
NORM_EPS = 1e-6
ROPE_BASE = 10000.0

SCAN_HEADS = 4
SCAN_DK = 128
SCAN_DV = 256
GLA_LOW_RANK = 16
GLA_TAU = 16.0

ATT_HEAD_DIM = 128
ATT_Q_HEADS = 8
ATT_KV_HEADS = 2
ATT_GROUP = ATT_Q_HEADS // ATT_KV_HEADS

MOE_GROUPS = 4
MOE_EPG = 8
MOE_EXPERTS = 32
MOE_BLOCK = 256
EXPERT_LANE0 = 32

LANES = 128
INPROJ_TN = 1024
NEG = -1e30

_REF_LAYOUT = (
    ('gla_q', 512), ('gla_k', 512), ('gla_v', 1024), ('gla_r', 1024), ('gla_lr', 32),
    ('att_q', 1024), ('att_k', 256), ('att_v', 256),
    ('ret_q', 512), ('ret_k', 512), ('ret_v', 1024), ('ret_g', 1024), ('gates', 3072),
)
_OUR_ORDER = ('gla_v', 'gla_r', 'att_q', 'ret_v', 'ret_g', 'gates', 'gla_q', 'gla_k', 'ret_q', 'ret_k',
              'att_k', 'att_v', 'gla_lr')


def _layout():
    ref_off, start = {}, 0
    for name, width in _REF_LAYOUT:
        ref_off[name] = (start, width)
        start += width
    col, off = {}, 0
    for name in _OUR_ORDER:
        col[name] = off
        off += ref_off[name][1]
    n_p = (off + INPROJ_TN - 1) // INPROJ_TN * INPROJ_TN
    return ref_off, col, n_p


_REF_OFF, COL, N_P = _layout()
VMEM_LIMIT = 56 * 1024 * 1024


def _cparams(sem):
    return pltpu.CompilerParams(dimension_semantics=sem, vmem_limit_bytes=VMEM_LIMIT)


def _silu(x):
    return x / (1.0 + jnp.exp(-x))


def _sigmoid(x):
    return 1.0 / (1.0 + jnp.exp(-x))


def _dot(a, b):
    return jnp.dot(a, b, preferred_element_type=F32)


def _dot_nt(a, b):
    return lax.dot_general(a, b, (((1,), (1,)), ((), ())), preferred_element_type=F32)


def _dot_tn(a, b):
    return lax.dot_general(a, b, (((0,), (0,)), ((), ())), preferred_element_type=F32)


def _mod_kernel(c_ref, w_ref, b_ref, o_ref):
    s = _silu(c_ref[...]).astype(BF16)
    o_ref[...] = _dot(s, w_ref[...].astype(BF16)) + b_ref[...]


def _modulation(cc, w_ada, b_ada):
    depth, d, n6 = w_ada.shape
    rows = cc.shape[0]
    tn = 512
    return pl.pallas_call(
        _mod_kernel,
        grid=(depth, n6 // tn),
        in_specs=[pl.BlockSpec((rows, d), lambda l, j: (0, 0)),
                  pl.BlockSpec((None, d, tn), lambda l, j: (l, 0, j)),
                  pl.BlockSpec((None, 1, tn), lambda l, j: (l, 0, j))],
        out_specs=pl.BlockSpec((None, rows, tn), lambda l, j: (l, 0, j)),
        out_shape=jax.ShapeDtypeStruct((depth, rows, n6), F32),
        compiler_params=_cparams(("arbitrary", "arbitrary")),
        name="modulation",
    )(cc, w_ada, b_ada.reshape(depth, 1, n6))


def _inproj_kernel(x_ref, mod_ref, g_ref, w_ref, o_ref, h_scr, *, n_ctx, tm):
    t = pl.program_id(1)

    @pl.when(pl.program_id(2) == 0)
    def _():
        x = x_ref[...]
        y = x * lax.rsqrt(jnp.mean(x * x, axis=-1, keepdims=True) + NORM_EPS) * g_ref[...]
        row = t * tm + lax.broadcasted_iota(jnp.int32, (tm, 1), 0)
        is_ctx = row < n_ctx
        shift = jnp.where(is_ctx, mod_ref[0, 0:1, :], mod_ref[1, 0:1, :])
        scale = jnp.where(is_ctx, mod_ref[0, 1:2, :], mod_ref[1, 1:2, :])
        h_scr[...] = (y * (1.0 + scale) + shift).astype(BF16)

    o_ref[...] = _dot(h_scr[...], w_ref[...]).astype(o_ref.dtype)


def _in_projection(xs, mod, g, w_p, n_ctx):
    b_, t_, d = xs.shape
    tm = 768 if t_ % 768 == 0 else 256
    tn = INPROJ_TN
    return pl.pallas_call(
        functools.partial(_inproj_kernel, n_ctx=n_ctx, tm=tm),
        grid=(b_, t_ // tm, N_P // tn),
        in_specs=[pl.BlockSpec((None, tm, d), lambda b, t, j: (b, t, 0)),
                  pl.BlockSpec((None, 2, 6, d), lambda b, t, j: (b, 0, 0, 0)),
                  pl.BlockSpec((1, d), lambda b, t, j: (0, 0)),
                  pl.BlockSpec((d, tn), lambda b, t, j: (0, j))],
        out_specs=pl.BlockSpec((None, tm, tn), lambda b, t, j: (b, t, j)),
        out_shape=jax.ShapeDtypeStruct((b_, t_, N_P), BF16),
        scratch_shapes=[pltpu.VMEM((tm, d), BF16)],
        compiler_params=_cparams(("arbitrary", "arbitrary", "arbitrary")),
        name="in_projection",
    )(xs, mod, g.reshape(1, d), w_p)


def _chunk_update(q_in, k_in, k_out, v_bf, dec, s_ref, h, mask):
    qb = q_in.astype(BF16)
    sc = jnp.where(mask, _dot_nt(qb, k_in.astype(BF16)), 0.0)
    st = s_ref[h]
    o = _dot(sc.astype(BF16), v_bf) + _dot_nt(qb, st.astype(BF16))
    s_ref[h] = st * dec + _dot_tn(v_bf, k_out.astype(BF16))
    return o


def _split3(x):
    hi = x.astype(BF16)
    r = x - hi.astype(F32)
    mid = r.astype(BF16)
    lo = (r - mid.astype(F32)).astype(BF16)
    return hi, mid, lo


def _group_norm_gate(o, g, r):
    mu = jnp.mean(o, axis=-1, keepdims=True)
    oc = o - mu
    var = jnp.mean(oc * oc, axis=-1, keepdims=True)
    return oc * lax.rsqrt(var + NORM_EPS) * g * _silu(r)


def _scan_kernel(*refs, backward):
    if backward:
        (gq, gk, gv, lr, rq, rk, rv, cos, sin, wa2, ba, obg, obr, sg, sr) = refs
    else:
        (gq, gk, gv, lr, rq, rk, rv, cos, sin, wa2, ba, gr, rg, obg, obr, gng, rng, yg, yr, sg, sr) = refs
    c_ = CHUNK

    @pl.when(pl.program_id(1) == 0)
    def _():
        sg[...] = jnp.zeros_like(sg)
        sr[...] = jnp.zeros_like(sr)

    row = lax.broadcasted_iota(jnp.int32, (c_, c_), 0)
    col = lax.broadcasted_iota(jnp.int32, (c_, c_), 1)
    if backward:
        tri = jnp.where(col >= row, 1.0, 0.0).astype(BF16)
        mask = col > row
    else:
        tri = jnp.where(col <= row, 1.0, 0.0).astype(BF16)
        mask = col <= row
    last = 0 if backward else c_ - 1

    pre = _dot(lr[...], wa2[...]) + ba[...]
    la = (jnp.minimum(pre, 0.0) - jnp.log1p(jnp.exp(-jnp.abs(pre)))) * (1.0 / GLA_TAU)
    hi, mid, lo = _split3(la)
    cum = _dot(tri, hi) + _dot(tri, mid) + _dot(tri, lo)
    tot = cum[last:last + 1, :]

    pos = lax.broadcasted_iota(jnp.int32, (c_, SCAN_DK), 0)
    steps = ((c_ - pos) if backward else (pos + 1)).astype(F32)
    cosv = cos[...]
    sinv = sin[...]
    scale = SCAN_DK ** -0.5

    for h in range(SCAN_HEADS):
        ks = slice(h * SCAN_DK, (h + 1) * SCAN_DK)
        vs = slice(h * SCAN_DV, (h + 1) * SCAN_DV)

        c = cum[:, ks]
        th = tot[:, ks]
        q = gq[:, ks].astype(F32) * scale
        k = gk[:, ks].astype(F32)
        o = _chunk_update(q * jnp.exp(c), k * jnp.exp(-c), k * jnp.exp(th - c), gv[:, vs], jnp.exp(th), sg, h, mask)
        if backward:
            obg[:, vs] = o
        else:
            yg[:, vs] = _group_norm_gate(o + obg[:, vs], gng[:, vs], gr[:, vs].astype(F32)).astype(yg.dtype)

        ld = math.log(1.0 - 2.0 ** (-5.0 - h))
        c = steps * ld
        q = rq[:, ks].astype(F32)
        k = rk[:, ks].astype(F32)
        q = q * cosv + pltpu.roll(q, SCAN_DK // 2, 1) * sinv
        k = (k * cosv + pltpu.roll(k, SCAN_DK // 2, 1) * sinv) * scale
        dec = jnp.full((1, SCAN_DK), math.exp(c_ * ld), F32)
        o = _chunk_update(q * jnp.exp(c), k * jnp.exp(-c), k * jnp.exp(c_ * ld - c), rv[:, vs], dec, sr, h, mask)
        if backward:
            obr[:, vs] = o
        else:
            yr[:, vs] = _group_norm_gate(o + obr[:, vs], rng[:, vs], rg[:, vs].astype(F32)).astype(yr.dtype)


def _scan(p, ret_cos, ret_sin, wa2_pad, ba, n_ctx, backward, extras=None):
    b_, t_, _ = p.shape
    n_chunks = t_ // CHUNK
    n_cc = n_ctx // CHUNK
    d = D_MODEL

    if backward:
        def cidx(s):
            return jnp.where(s < n_cc, n_cc - 1 - s, n_chunks - 1 + n_cc - s)
    else:
        def cidx(s):
            return s

    def pspec(name, width):
        blk = COL[name] // width
        return pl.BlockSpec((None, CHUNK, width), lambda b, s: (b, cidx(s), blk))

    dirn = 1 if backward else 0
    tok = lambda width: pl.BlockSpec((None, CHUNK, width), lambda b, s: (b, cidx(s), 0))
    in_specs = [pspec('gla_q', 512), pspec('gla_k', 512), pspec('gla_v', 1024), pspec('gla_lr', LANES),
                pspec('ret_q', 512), pspec('ret_k', 512), pspec('ret_v', 1024),
                pl.BlockSpec((CHUNK, SCAN_DK), lambda b, s: (cidx(s), 0)),
                pl.BlockSpec((CHUNK, SCAN_DK), lambda b, s: (cidx(s), 0)),
                pl.BlockSpec((None, LANES, 512), lambda b, s: (dirn, 0, 0)),
                pl.BlockSpec((None, 1, 512), lambda b, s: (dirn, 0, 0))]
    args = [p, p, p, p, p, p, p, ret_cos, ret_sin, wa2_pad, ba]
    state = [pltpu.VMEM((SCAN_HEADS, SCAN_DV, SCAN_DK), F32), pltpu.VMEM((SCAN_HEADS, SCAN_DV, SCAN_DK), F32)]
    if backward:
        out_specs = [tok(d), tok(d)]
        out_shape = [jax.ShapeDtypeStruct((b_, t_, d), F32)] * 2
    else:
        obg, obr, gng, rng = extras
        in_specs += [pspec('gla_r', 1024), pspec('ret_g', 1024), tok(d), tok(d),
                     pl.BlockSpec((1, d), lambda b, s: (0, 0)), pl.BlockSpec((1, d), lambda b, s: (0, 0))]
        args += [p, p, obg, obr, gng.reshape(1, d), rng.reshape(1, d)]
        out_specs = [tok(d), tok(d)]
        out_shape = [jax.ShapeDtypeStruct((b_, t_, d), BF16)] * 2
    return pl.pallas_call(
        functools.partial(_scan_kernel, backward=backward),
        grid=(b_, n_chunks),
        in_specs=in_specs, out_specs=out_specs, out_shape=out_shape,
        scratch_shapes=state,
        compiler_params=_cparams(("arbitrary", "arbitrary")),
        name="scan_bwd" if backward else "scan_fwd",
    )(*args)


def _rope_axial(x, cosv, sinv, lane_lo):
    partner = jnp.where(lane_lo, pltpu.roll(x, 96, 1), pltpu.roll(x, 32, 1))
    return x * cosv + partner * sinv


def _attn_block(refs, local, first, last):
    (sink, q, kp, ks, kn, vp, vs, vn, kc, vc, cp, cs, cn, sp, ss, sn, o_ref) = refs
    qb_ = q.shape[0]
    hd = ATT_HEAD_DIM
    scale = hd ** -0.5
    rows = ATT_GROUP * qb_
    rowi = lax.broadcasted_iota(jnp.int32, (rows, 1), 0)
    if local:
        lane = lax.broadcasted_iota(jnp.int32, (qb_, hd), 1)
        lane_lo = (lane & 63) < 32
        i_idx = lax.broadcasted_iota(jnp.int32, (rows, qb_), 0) & (qb_ - 1)
        j_idx = lax.broadcasted_iota(jnp.int32, (rows, qb_), 1)
        mask_p = jnp.logical_and(j_idx >= i_idx, jnp.logical_not(first))
        mask_n = jnp.logical_and(j_idx <= i_idx, jnp.logical_not(last))

    for g in range(ATT_KV_HEADS):
        gs = slice(g * hd, (g + 1) * hd)
        heads = []
        for hh in range(ATT_GROUP):
            qs = slice((g * ATT_GROUP + hh) * hd, (g * ATT_GROUP + hh + 1) * hd)
            qh = q[:, qs].astype(F32)
            if local:
                qh = _rope_axial(qh, cs[...], ss[...], lane_lo)
            heads.append(qh.astype(BF16))
        q4 = jnp.concatenate(heads, axis=0)
        sink_col = jnp.full((rows, 1), sink[g * ATT_GROUP], F32)
        for hh in range(1, ATT_GROUP):
            sink_col = jnp.where(rowi >= hh * qb_, sink[g * ATT_GROUP + hh], sink_col)

        s_c = _dot_nt(q4, kc[:, gs]) * scale
        m = jnp.maximum(sink_col, jnp.max(s_c, axis=-1, keepdims=True))
        if local:
            k_p = _rope_axial(kp[:, gs].astype(F32), cp[...], sp[...], lane_lo).astype(BF16)
            k_s = _rope_axial(ks[:, gs].astype(F32), cs[...], ss[...], lane_lo).astype(BF16)
            k_n = _rope_axial(kn[:, gs].astype(F32), cn[...], sn[...], lane_lo).astype(BF16)
            s_p = jnp.where(mask_p, _dot_nt(q4, k_p) * scale, NEG)
            s_s = _dot_nt(q4, k_s) * scale
            s_n = jnp.where(mask_n, _dot_nt(q4, k_n) * scale, NEG)
            m = jnp.maximum(m, jnp.max(s_p, axis=-1, keepdims=True))
            m = jnp.maximum(m, jnp.max(s_s, axis=-1, keepdims=True))
            m = jnp.maximum(m, jnp.max(s_n, axis=-1, keepdims=True))
        p_c = jnp.exp(s_c - m)
        den = jnp.exp(sink_col - m) + jnp.sum(p_c, axis=-1, keepdims=True)
        acc = _dot(p_c.astype(BF16), vc[:, gs])
        if local:
            p_p = jnp.exp(s_p - m)
            p_s = jnp.exp(s_s - m)
            p_n = jnp.exp(s_n - m)
            den = den + jnp.sum(p_p, axis=-1, keepdims=True) + jnp.sum(p_s, axis=-1, keepdims=True) \
                + jnp.sum(p_n, axis=-1, keepdims=True)
            acc = acc + _dot(p_p.astype(BF16), vp[:, gs]) + _dot(p_s.astype(BF16), vs[:, gs]) \
                + _dot(p_n.astype(BF16), vn[:, gs])
        out = acc / den
        for hh in range(ATT_GROUP):
            qs = slice((g * ATT_GROUP + hh) * hd, (g * ATT_GROUP + hh + 1) * hd)
            o_ref[:, qs] = out[hh * qb_:(hh + 1) * qb_, :].astype(o_ref.dtype)


def _attn_kernel(*refs, n_cc, nb):
    n = pl.program_id(1)

    @pl.when(n < n_cc)
    def _():
        _attn_block(refs, False, None, None)

    @pl.when(n >= n_cc)
    def _():
        _attn_block(refs, True, n == n_cc, n == n_cc + nb - 1)


def _attention(p, sink, att_cos, att_sin, n_ctx):
    b_, t_, _ = p.shape
    qb_ = 128
    n_cc = n_ctx // qb_
    nb = (t_ - n_ctx) // qb_
    d = D_MODEL
    qblk, kblk, vblk = COL['att_q'] // 1024, COL['att_k'] // 256, COL['att_v'] // 256
    smem = pl.BlockSpec(memory_space=pltpu.SMEM)

    prev = lambda n: jnp.maximum(n - 1, n_cc)
    this = lambda n: jnp.maximum(n, n_cc)
    nxt = lambda n: jnp.maximum(jnp.minimum(n + 1, n_cc + nb - 1), n_cc)

    def tspec(width, blk, fn):
        return pl.BlockSpec((None, qb_, width), lambda b, n: (b, fn(n), blk))

    def tab(fn):
        return pl.BlockSpec((qb_, ATT_HEAD_DIM), lambda b, n: (fn(n) - n_cc, 0))

    ctx_k = pl.BlockSpec((None, n_ctx, 256), lambda b, n: (b, 0, kblk))
    ctx_v = pl.BlockSpec((None, n_ctx, 256), lambda b, n: (b, 0, vblk))
    return pl.pallas_call(
        functools.partial(_attn_kernel, n_cc=n_cc, nb=nb),
        grid=(b_, n_cc + nb),
        in_specs=[smem, tspec(1024, qblk, lambda n: n),
                  tspec(256, kblk, prev), tspec(256, kblk, this), tspec(256, kblk, nxt),
                  tspec(256, vblk, prev), tspec(256, vblk, this), tspec(256, vblk, nxt),
                  ctx_k, ctx_v, tab(prev), tab(this), tab(nxt), tab(prev), tab(this), tab(nxt)],
        out_specs=pl.BlockSpec((None, qb_, d), lambda b, n: (b, n, 0)),
        out_shape=jax.ShapeDtypeStruct((b_, t_, d), BF16),
        compiler_params=_cparams(("arbitrary", "arbitrary")),
        name="attention",
    )(sink, p, p, p, p, p, p, p, p, p, att_cos, att_cos, att_cos, att_sin, att_sin, att_sin)


def _merge_kernel(yg, ya, yr, pg, pa, pr, x_ref, mod_ref, g2_ref, wg, wa, wr, wo, wrh, wrl, br,
                  xo_ref, h2_ref, route_ref, cnt_ref, cnt_scr, *, tm):
    first = jnp.logical_and(pl.program_id(0) == 0, pl.program_id(1) == 0)

    @pl.when(first)
    def _():
        cnt_scr[...] = jnp.zeros_like(cnt_scr)

    merged = (_sigmoid(pg[...].astype(F32)) * _dot(yg[...], wg[...])
              + _sigmoid(pa[...].astype(F32)) * _dot(ya[...], wa[...])
              + _sigmoid(pr[...].astype(F32)) * _dot(yr[...], wr[...]))
    mix = _dot(merged.astype(BF16), wo[...])
    x = x_ref[...] + mod_ref[2:3, :] * mix
    xo_ref[...] = x
    y = x * lax.rsqrt(jnp.mean(x * x, axis=-1, keepdims=True) + NORM_EPS) * g2_ref[...]
    h2 = y * (1.0 + mod_ref[4:5, :]) + mod_ref[3:4, :]
    h2_ref[...] = h2

    h_hi = h2.astype(BF16)
    h_lo = (h2 - h_hi.astype(F32)).astype(BF16)
    logits = _dot(h_hi, wrh[...]) + _dot(h_lo, wrh[...]) + _dot(h_hi, wrl[...]) + br[...]

    lane = lax.broadcasted_iota(jnp.int32, (tm, LANES), 1)
    lanef = lane.astype(F32)
    is_grp = lane < MOE_GROUPS
    gl = jnp.where(is_grp, logits, NEG)
    gmax = jnp.max(gl, axis=-1, keepdims=True)
    gidx = jnp.min(jnp.where(gl == gmax, lanef, float(LANES)), axis=-1, keepdims=True)
    gprob = 1.0 / jnp.sum(jnp.where(is_grp, jnp.exp(gl - gmax), 0.0), axis=-1, keepdims=True)
    lo = EXPERT_LANE0 + MOE_EPG * gidx
    el = jnp.where(jnp.logical_and(lanef >= lo, lanef < lo + MOE_EPG), logits, NEG)
    t1 = jnp.max(el, axis=-1, keepdims=True)
    i1 = jnp.min(jnp.where(el == t1, lanef, float(LANES)), axis=-1, keepdims=True)
    el2 = jnp.where(lanef == i1, NEG, el)
    t2 = jnp.max(el2, axis=-1, keepdims=True)
    i2 = jnp.min(jnp.where(el2 == t2, lanef, float(LANES)), axis=-1, keepdims=True)
    ex = jnp.exp(t2 - t1)
    w0 = gprob / (1.0 + ex)
    w1 = gprob * ex / (1.0 + ex)

    oh0 = lanef == i1
    oh1 = lanef == i2
    both = jnp.where(jnp.logical_or(oh0, oh1), 1.0, 0.0)
    rr = lax.broadcasted_iota(jnp.int32, (tm, tm), 0)
    cc = lax.broadcasted_iota(jnp.int32, (tm, tm), 1)
    tri = jnp.where(cc < rr, 1.0, 0.0).astype(BF16)
    base = cnt_scr[...] + _dot(tri, both.astype(BF16))
    r0 = jnp.sum(jnp.where(oh0, base, 0.0), axis=-1, keepdims=True)
    r1 = jnp.sum(jnp.where(oh1, base, 0.0), axis=-1, keepdims=True)
    cnt = cnt_scr[...] + jnp.sum(both, axis=0, keepdims=True)
    cnt_scr[...] = cnt
    cnt_ref[...] = jnp.broadcast_to(cnt, cnt_ref.shape)

    out = jnp.where(lane == 0, i1 - EXPERT_LANE0, 0.0)
    out = jnp.where(lane == 1, i2 - EXPERT_LANE0, out)
    out = jnp.where(lane == 2, w0, out)
    out = jnp.where(lane == 3, w1, out)
    out = jnp.where(lane == 4, r0, out)
    out = jnp.where(lane == 5, r1, out)
    route_ref[...] = out


def _merge(yg, ya, yr, p, xs, mod, g2, wg, wa, wr, wo, wrh, wrl, br, n_ctx):
    b_, t_, d = xs.shape
    tm = 256
    nt = t_ // tm
    n_ct = n_ctx // tm
    gblk = COL['gates'] // d
    tok = pl.BlockSpec((None, tm, d), lambda b, t: (b, t, 0))
    wspec = pl.BlockSpec((d, d), lambda b, t: (0, 0))
    rspec = pl.BlockSpec((d, LANES), lambda b, t: (0, 0))

    def gate(i):
        return pl.BlockSpec((None, tm, d), lambda b, t: (b, t, gblk + i))

    return pl.pallas_call(
        functools.partial(_merge_kernel, tm=tm),
        grid=(b_, nt),
        in_specs=[tok, tok, tok, gate(0), gate(1), gate(2), tok,
                  pl.BlockSpec((None, None, 6, d), lambda b, t: (b, jnp.where(t < n_ct, 0, 1), 0, 0)),
                  pl.BlockSpec((1, d), lambda b, t: (0, 0)),
                  wspec, wspec, wspec, wspec, rspec, rspec,
                  pl.BlockSpec((1, LANES), lambda b, t: (0, 0))],
        out_specs=[tok, tok,
                   pl.BlockSpec((None, tm, LANES), lambda b, t: (b, t, 0)),
                   pl.BlockSpec((8, LANES), lambda b, t: (0, 0))],
        out_shape=[jax.ShapeDtypeStruct((b_, t_, d), F32), jax.ShapeDtypeStruct((b_, t_, d), F32),
                   jax.ShapeDtypeStruct((b_, t_, LANES), F32), jax.ShapeDtypeStruct((8, LANES), F32)],
        scratch_shapes=[pltpu.VMEM((1, LANES), F32)],
        compiler_params=_cparams(("arbitrary", "arbitrary")),
        name="merge_route",
    )(yg, ya, yr, p, p, p, xs, mod, g2.reshape(1, d), wg, wa, wr, wo, wrh, wrl, br)


def _dispatch_kernel(dest_ref, h_ref, init_ref, xs_ref, sem, *, tm):
    del init_ref

    def row_copy(i, k):
        return pltpu.make_async_copy(h_ref.at[pl.ds(i, 1)], xs_ref.at[pl.ds(dest_ref[0, k * tm + i], 1)], sem)

    def issue(i, carry):
        row_copy(i, 0).start()
        row_copy(i, 1).start()
        return carry

    lax.fori_loop(0, tm, issue, 0)

    def drain(i, carry):
        row_copy(i, 0).wait()
        row_copy(i, 1).wait()
        return carry

    lax.fori_loop(0, tm, drain, 0)


def _dispatch(h2, dest, n_slots):
    b_, t_, d = h2.shape
    tm = 256
    nt = t_ // tm
    return pl.pallas_call(
        functools.partial(_dispatch_kernel, tm=tm),
        grid=(b_, nt),
        in_specs=[pl.BlockSpec((None, 1, 2 * tm), lambda b, t: (b * nt + t, 0, 0), memory_space=pltpu.SMEM),
                  pl.BlockSpec((tm, d), lambda b, t: (b * nt + t, 0)),
                  pl.BlockSpec(memory_space=pl.ANY)],
        out_specs=pl.BlockSpec(memory_space=pl.ANY),
        out_shape=jax.ShapeDtypeStruct((n_slots, d), F32),
        input_output_aliases={2: 0},
        scratch_shapes=[pltpu.SemaphoreType.DMA],
        compiler_params=_cparams(("arbitrary", "arbitrary")),
        name="moe_dispatch",
    )(dest, h2.reshape(b_ * t_, d), jnp.zeros((n_slots, d), F32))


def _expert_kernel(be_ref, nu_ref, x_ref, w1_ref, w3_ref, w2_ref, y_ref):
    del be_ref
    i = pl.program_id(0)

    @pl.when(i < nu_ref[0])
    def _():
        xb = x_ref[...].astype(BF16)
        h1 = _dot(xb, w1_ref[...])
        h3 = _dot(xb, w3_ref[...])
        y_ref[...] = _dot((_silu(h1) * h3).astype(BF16), w2_ref[...])

    @pl.when(i >= nu_ref[0])
    def _():
        y_ref[...] = jnp.zeros_like(y_ref)


def _experts(x_sorted, blk_exp, n_used, w1, w3, w2):
    n_slots, d = x_sorted.shape
    hid = w1.shape[-1]
    n_blocks = n_slots // MOE_BLOCK
    grid_spec = pltpu.PrefetchScalarGridSpec(
        num_scalar_prefetch=2,
        grid=(n_blocks,),
        in_specs=[pl.BlockSpec((MOE_BLOCK, d), lambda i, be, nu: (i, 0)),
                  pl.BlockSpec((None, d, hid), lambda i, be, nu: (be[i], 0, 0)),
                  pl.BlockSpec((None, d, hid), lambda i, be, nu: (be[i], 0, 0)),
                  pl.BlockSpec((None, hid, d), lambda i, be, nu: (be[i], 0, 0))],
        out_specs=pl.BlockSpec((MOE_BLOCK, d), lambda i, be, nu: (i, 0)),
    )
    return pl.pallas_call(
        _expert_kernel,
        grid_spec=grid_spec,
        out_shape=jax.ShapeDtypeStruct((n_slots, d), F32),
        compiler_params=_cparams(("arbitrary",)),
        name="moe_experts",
    )(blk_exp, n_used, x_sorted, w1, w3, w2)


def _combine_kernel(dest_ref, x_ref, route_ref, mod_ref, fg_ref, y_ref, o_ref, buf, sem, *, tm, final):
    def row_copy(i, k):
        return pltpu.make_async_copy(y_ref.at[pl.ds(dest_ref[0, k * tm + i], 1)], buf.at[k, pl.ds(i, 1)], sem)

    def issue(i, carry):
        row_copy(i, 0).start()
        row_copy(i, 1).start()
        return carry

    lax.fori_loop(0, tm, issue, 0)

    def drain(i, carry):
        row_copy(i, 0).wait()
        row_copy(i, 1).wait()
        return carry

    lax.fori_loop(0, tm, drain, 0)

    r = route_ref[...]
    f = buf[0] * r[:, 2:3] + buf[1] * r[:, 3:4]
    x = x_ref[...] + mod_ref[5:6, :] * f
    if final:
        x = x * lax.rsqrt(jnp.mean(x * x, axis=-1, keepdims=True) + NORM_EPS) * fg_ref[...]
    o_ref[...] = x


def _combine(xs, route, mod, final_g, y_sorted, dest, n_ctx, final):
    b_, t_, d = xs.shape
    tm = 256
    nt = t_ // tm
    n_ct = n_ctx // tm
    skip = n_ct if final else 0
    tok = lambda w: pl.BlockSpec((None, tm, w), lambda b, t: (b, t + skip, 0))
    return pl.pallas_call(
        functools.partial(_combine_kernel, tm=tm, final=final),
        grid=(b_, nt - skip),
        in_specs=[pl.BlockSpec((None, 1, 2 * tm), lambda b, t: (b * nt + t + skip, 0, 0), memory_space=pltpu.SMEM),
                  tok(d), tok(LANES),
                  pl.BlockSpec((None, None, 6, d), lambda b, t: (b, jnp.where(t + skip < n_ct, 0, 1), 0, 0)),
                  pl.BlockSpec((1, d), lambda b, t: (0, 0)),
                  pl.BlockSpec(memory_space=pl.ANY)],
        out_specs=pl.BlockSpec((None, tm, d), lambda b, t: (b, t, 0)),
        out_shape=jax.ShapeDtypeStruct((b_, t_ - skip * tm, d), F32),
        scratch_shapes=[pltpu.VMEM((2, tm, d), F32), pltpu.SemaphoreType.DMA],
        compiler_params=_cparams(("arbitrary", "arbitrary")),
        name="moe_combine",
    )(dest, xs, route, mod, final_g.reshape(1, d), y_sorted)


def _slot_plan(route, counts, tm):
    b_, t_, _ = route.shape
    n_tok = b_ * t_
    cnt = counts[0, EXPERT_LANE0:EXPERT_LANE0 + MOE_EXPERTS].astype(jnp.int32)
    padded = (cnt + MOE_BLOCK - 1) // MOE_BLOCK * MOE_BLOCK
    pad_end = jnp.cumsum(padded)
    pad_start = pad_end - padded
    n_blocks = (n_tok * 2 + MOE_EXPERTS * (MOE_BLOCK - 1) + MOE_BLOCK - 1) // MOE_BLOCK
    n_used = (pad_end[-1] // MOE_BLOCK).astype(jnp.int32)
    blk = jnp.arange(n_blocks, dtype=jnp.int32)
    blk_exp = jnp.searchsorted(pad_end, jnp.minimum(blk, n_used - 1) * MOE_BLOCK, side='right')
    blk_exp = jnp.minimum(blk_exp, MOE_EXPERTS - 1).astype(jnp.int32)
    e = route[..., 0:2].astype(jnp.int32)
    rank = route[..., 4:6].astype(jnp.int32)
    dest = pad_start[e] + rank
    dest = dest.reshape(b_ * t_ // tm, tm, 2).transpose(0, 2, 1).reshape(b_ * t_ // tm, 1, 2 * tm)
    return dest, blk_exp, n_used.reshape(1), n_blocks * MOE_BLOCK


def _permute_w_in(w_in):
    parts = [w_in[:, _REF_OFF[name][0]:_REF_OFF[name][0] + _REF_OFF[name][1]] for name in _OUR_ORDER]
    width = sum(part.shape[1] for part in parts)
    parts.append(jnp.zeros((w_in.shape[0], N_P - width), w_in.dtype))
    return jnp.concatenate(parts, axis=1).astype(BF16)


def _rope_tables(n_ctx, l_):
    half = SCAN_DK // 2
    inv = ROPE_BASE ** (-jnp.arange(half, dtype=F32) / half)
    ang = jnp.arange(n_ctx + l_, dtype=F32)[:, None] * inv[None, :]
    ret_cos = jnp.concatenate([jnp.cos(ang), jnp.cos(ang)], axis=-1)
    ret_sin = jnp.concatenate([-jnp.sin(ang), jnp.sin(ang)], axis=-1)
    q = ATT_HEAD_DIM // 4
    inv = ROPE_BASE ** (-jnp.arange(q, dtype=F32) / q)
    tpos = jnp.arange(l_)
    a_r = (tpos // GRID_W).astype(F32)[:, None] * inv[None, :]
    a_c = (tpos % GRID_W).astype(F32)[:, None] * inv[None, :]
    att_cos = jnp.concatenate([jnp.cos(a_r), jnp.cos(a_r), jnp.cos(a_c), jnp.cos(a_c)], axis=-1)
    att_sin = jnp.concatenate([-jnp.sin(a_r), jnp.sin(a_r), -jnp.sin(a_c), jnp.sin(a_c)], axis=-1)
    return ret_cos, ret_sin, att_cos, att_sin


def kernel(x, c, ctx, c_ctx, w_ada, b_ada, norm1_g, norm2_g, w_in, gla_wa2, gla_ba, gla_norm_g, attn_sink,
           ret_norm_g, w_br_gla, w_br_attn, w_br_ret, w_out, moe_w_grp, moe_b_grp, moe_w_exp, moe_b_exp,
           moe_w1, moe_w3, moe_w2, final_g):
    b_, l_, d = x.shape
    n_ctx = ctx.shape[1]
    depth = w_ada.shape[0]
    assert d == D_MODEL and n_ctx % 256 == 0 and l_ % 256 == 0 and b_ <= 8

    xs = jnp.concatenate([ctx, x], axis=1)
    cc = jnp.zeros((16, d), F32).at[:b_].set(c).at[b_].set(c_ctx)
    mod_all = _modulation(cc, w_ada, b_ada)
    ret_cos, ret_sin, att_cos, att_sin = _rope_tables(n_ctx, l_)

    for layer in range(depth):
        last = layer == depth - 1
        m = mod_all[layer].reshape(16, 6, d)
        mod = jnp.stack([jnp.broadcast_to(m[b_], (b_, 6, d)), m[:b_]], axis=1)

        w_p = _permute_w_in(w_in[layer])
        wa2_pad = jnp.zeros((2, LANES, 512), F32)
        wa2_pad = wa2_pad.at[0, 0:GLA_LOW_RANK].set(gla_wa2[layer, 0])
        wa2_pad = wa2_pad.at[1, GLA_LOW_RANK:2 * GLA_LOW_RANK].set(gla_wa2[layer, 1]).astype(BF16)
        ba = gla_ba[layer].reshape(2, 1, 512)

        p = _in_projection(xs, mod, norm1_g[layer], w_p, n_ctx)
        obg, obr = _scan(p, ret_cos, ret_sin, wa2_pad, ba, n_ctx, True)
        yg, yr = _scan(p, ret_cos, ret_sin, wa2_pad, ba, n_ctx, False,
                       (obg, obr, gla_norm_g[layer], ret_norm_g[layer]))
        ya = _attention(p, attn_sink[layer], att_cos, att_sin, n_ctx)

        w_route = jnp.zeros((d, LANES), F32)
        w_route = w_route.at[:, 0:MOE_GROUPS].set(moe_w_grp[layer])
        w_route = w_route.at[:, EXPERT_LANE0:EXPERT_LANE0 + MOE_EXPERTS].set(moe_w_exp[layer])
        wr_hi = w_route.astype(BF16)
        wr_lo = (w_route - wr_hi.astype(F32)).astype(BF16)
        b_route = jnp.zeros((1, LANES), F32)
        b_route = b_route.at[0, 0:MOE_GROUPS].set(moe_b_grp[layer])
        b_route = b_route.at[0, EXPERT_LANE0:EXPERT_LANE0 + MOE_EXPERTS].set(moe_b_exp[layer])

        xs, h2, route, counts = _merge(
            yg, ya, yr, p, xs, mod, norm2_g[layer],
            w_br_gla[layer].astype(BF16), w_br_attn[layer].astype(BF16), w_br_ret[layer].astype(BF16),
            w_out[layer].astype(BF16), wr_hi, wr_lo, b_route, n_ctx)

        dest, blk_exp, n_used, n_slots = _slot_plan(route, counts, 256)
        x_sorted = _dispatch(h2, dest, n_slots)
        y_sorted = _experts(x_sorted, blk_exp, n_used, moe_w1[layer].astype(BF16), moe_w3[layer].astype(BF16),
                            moe_w2[layer].astype(BF16))
        xs = _combine(xs, route, mod, final_g, y_sorted, dest, n_ctx, last)

    return xs
```

```python
import functools
import math

import jax
import jax.numpy as jnp
from jax import lax
from jax.experimental import pallas as pl
from jax.experimental.pallas import tpu as pltpu

F32 = jnp.float32
BF16 = jnp.bfloat16

D_MODEL = 1024
GRID_W = 64
CHUNK = 128
NORM_EPS = 1e-6
ROPE_BASE = 10000.0

SCAN_HEADS = 4
SCAN_DK = 128
SCAN_DV = 256
GLA_LOW_RANK = 16
GLA_TAU = 16.0

ATT_HEAD_DIM = 128
ATT_Q_HEADS = 8
ATT_KV_HEADS = 2
ATT_GROUP = ATT_Q_HEADS // ATT_KV_HEADS

MOE_GROUPS = 4
MOE_EPG = 8
MOE_EXPERTS = 32
MOE_BLOCK = 256
EXPERT_LANE0 = 32

LANES = 128
INPROJ_TN = 1024
NEG = -1e30

_REF_LAYOUT = (
    ('gla_q', 512), ('gla_k', 512), ('gla_v', 1024), ('gla_r', 1024), ('gla_lr', 32),
    ('att_q', 1024), ('att_k', 256), ('att_v', 256),
    ('ret_q', 512), ('ret_k', 512), ('ret_v', 1024), ('ret_g', 1024), ('gates', 3072),
)
_OUR_ORDER = ('gla_v', 'gla_r', 'att_q', 'ret_v', 'ret_g', 'gates', 'gla_q', 'gla_k', 'ret_q', 'ret_k',
              'att_k', 'att_v', 'gla_lr')


def _layout():
    ref_off, start = {}, 0
    for name, width in _REF_LAYOUT:
        ref_off[name] = (start, width)
        start += width
    col, off = {}, 0
    for name in _OUR_ORDER:
        col[name] = off
        off += ref_off[name][1]
    n_p = (off + INPROJ_TN - 1) // INPROJ_TN * INPROJ_TN
    return ref_off, col, n_p


_REF_OFF, COL, N_P = _layout()
VMEM_LIMIT = 56 * 1024 * 1024


def _cparams(sem):
    return pltpu.CompilerParams(dimension_semantics=sem, vmem_limit_bytes=VMEM_LIMIT)


def _silu(x):
    return x / (1.0 + jnp.exp(-x))


def _sigmoid(x):
    return 1.0 / (1.0 + jnp.exp(-x))


def _dot(a, b):
    return jnp.dot(a, b, preferred_element_type=F32)


def _dot_nt(a, b):
    return lax.dot_general(a, b, (((1,), (1,)), ((), ())), preferred_element_type=F32)


def _dot_tn(a, b):
    return lax.dot_general(a, b, (((0,), (0,)), ((), ())), preferred_element_type=F32)


def _mod_kernel(c_ref, w_ref, b_ref, o_ref):
    s = _silu(c_ref[...]).astype(BF16)
    o_ref[...] = _dot(s, w_ref[...].astype(BF16)) + b_ref[...]


def _modulation(cc, w_ada, b_ada):
    depth, d, n6 = w_ada.shape
    rows = cc.shape[0]
    tn = 512
    return pl.pallas_call(
        _mod_kernel,
        grid=(depth, n6 // tn),
        in_specs=[pl.BlockSpec((rows, d), lambda l, j: (0, 0)),
                  pl.BlockSpec((None, d, tn), lambda l, j: (l, 0, j)),
                  pl.BlockSpec((None, 1, tn), lambda l, j: (l, 0, j))],
        out_specs=pl.BlockSpec((None, rows, tn), lambda l, j: (l, 0, j)),
        out_shape=jax.ShapeDtypeStruct((depth, rows, n6), F32),
        compiler_params=_cparams(("arbitrary", "arbitrary")),
        name="modulation",
    )(cc, w_ada, b_ada.reshape(depth, 1, n6))


def _inproj_kernel(x_ref, mod_ref, g_ref, w_ref, o_ref, h_scr, *, n_ctx, tm):
    t = pl.program_id(1)

    @pl.when(pl.program_id(2) == 0)
    def _():
        x = x_ref[...]
        y = x * lax.rsqrt(jnp.mean(x * x, axis=-1, keepdims=True) + NORM_EPS) * g_ref[...]
        row = t * tm + lax.broadcasted_iota(jnp.int32, (tm, 1), 0)
        is_ctx = row < n_ctx
        shift = jnp.where(is_ctx, mod_ref[0, 0:1, :], mod_ref[1, 0:1, :])
        scale = jnp.where(is_ctx, mod_ref[0, 1:2, :], mod_ref[1, 1:2, :])
        h_scr[...] = (y * (1.0 + scale) + shift).astype(BF16)

    o_ref[...] = _dot(h_scr[...], w_ref[...]).astype(o_ref.dtype)


def _in_projection(xs, mod, g, w_p, n_ctx):
    b_, t_, d = xs.shape
    tm = 768 if t_ % 768 == 0 else 256
    tn = INPROJ_TN
    return pl.pallas_call(
        functools.partial(_inproj_kernel, n_ctx=n_ctx, tm=tm),
        grid=(b_, t_ // tm, N_P // tn),
        in_specs=[pl.BlockSpec((None, tm, d), lambda b, t, j: (b, t, 0)),
                  pl.BlockSpec((None, 2, 6, d), lambda b, t, j: (b, 0, 0, 0)),
                  pl.BlockSpec((1, d), lambda b, t, j: (0, 0)),
                  pl.BlockSpec((d, tn), lambda b, t, j: (0, j))],
        out_specs=pl.BlockSpec((None, tm, tn), lambda b, t, j: (b, t, j)),
        out_shape=jax.ShapeDtypeStruct((b_, t_, N_P), BF16),
        scratch_shapes=[pltpu.VMEM((tm, d), BF16)],
        compiler_params=_cparams(("arbitrary", "arbitrary", "arbitrary")),
        name="in_projection",
    )(xs, mod, g.reshape(1, d), w_p)


def _chunk_update(q_in, k_in, k_out, v_bf, dec, s_ref, h, mask):
    qb = q_in.astype(BF16)
    sc = jnp.where(mask, _dot_nt(qb, k_in.astype(BF16)), 0.0)
    st = s_ref[h]
    o = _dot(sc.astype(BF16), v_bf) + _dot_nt(qb, st.astype(BF16))
    s_ref[h] = st * dec + _dot_tn(v_bf, k_out.astype(BF16))
    return o


def _split3(x):
    hi = x.astype(BF16)
    r = x - hi.astype(F32)
    mid = r.astype(BF16)
    lo = (r - mid.astype(F32)).astype(BF16)
    return hi, mid, lo


def _group_norm_gate(o, g, r):
    mu = jnp.mean(o, axis=-1, keepdims=True)
    oc = o - mu
    var = jnp.mean(oc * oc, axis=-1, keepdims=True)
    return oc * lax.rsqrt(var + NORM_EPS) * g * _silu(r)


def _scan_kernel(*refs, backward):
    if backward:
        (gq, gk, gv, lr, rq, rk, rv, cos, sin, wa2, ba, obg, obr, sg, sr) = refs
    else:
        (gq, gk, gv, lr, rq, rk, rv, cos, sin, wa2, ba, gr, rg, obg, obr, gng, rng, yg, yr, sg, sr) = refs
    c_ = CHUNK

    @pl.when(pl.program_id(1) == 0)
    def _():
        sg[...] = jnp.zeros_like(sg)
        sr[...] = jnp.zeros_like(sr)

    row = lax.broadcasted_iota(jnp.int32, (c_, c_), 0)
    col = lax.broadcasted_iota(jnp.int32, (c_, c_), 1)
    if backward:
        tri = jnp.where(col >= row, 1.0, 0.0).astype(BF16)
        mask = col > row
    else:
        tri = jnp.where(col <= row, 1.0, 0.0).astype(BF16)
        mask = col <= row
    last = 0 if backward else c_ - 1

    pre = _dot(lr[...], wa2[...]) + ba[...]
    la = (jnp.minimum(pre, 0.0) - jnp.log1p(jnp.exp(-jnp.abs(pre)))) * (1.0 / GLA_TAU)
    hi, mid, lo = _split3(la)
    cum = _dot(tri, hi) + _dot(tri, mid) + _dot(tri, lo)
    tot = cum[last:last + 1, :]

    pos = lax.broadcasted_iota(jnp.int32, (c_, SCAN_DK), 0)
    steps = ((c_ - pos) if backward else (pos + 1)).astype(F32)
    cosv = cos[...]
    sinv = sin[...]
    scale = SCAN_DK ** -0.5

    for h in range(SCAN_HEADS):
        ks = slice(h * SCAN_DK, (h + 1) * SCAN_DK)
        vs = slice(h * SCAN_DV, (h + 1) * SCAN_DV)

        c = cum[:, ks]
        th = tot[:, ks]
        q = gq[:, ks].astype(F32) * scale
        k = gk[:, ks].astype(F32)
        o = _chunk_update(q * jnp.exp(c), k * jnp.exp(-c), k * jnp.exp(th - c), gv[:, vs], jnp.exp(th), sg, h, mask)
        if backward:
            obg[:, vs] = o
        else:
            yg[:, vs] = _group_norm_gate(o + obg[:, vs], gng[:, vs], gr[:, vs].astype(F32)).astype(yg.dtype)

        ld = math.log(1.0 - 2.0 ** (-5.0 - h))
        c = steps * ld
        q = rq[:, ks].astype(F32)
        k = rk[:, ks].astype(F32)
        q = q * cosv + pltpu.roll(q, SCAN_DK // 2, 1) * sinv
        k = (k * cosv + pltpu.roll(k, SCAN_DK // 2, 1) * sinv) * scale
        dec = jnp.full((1, SCAN_DK), math.exp(c_ * ld), F32)
        o = _chunk_update(q * jnp.exp(c), k * jnp.exp(-c), k * jnp.exp(c_ * ld - c), rv[:, vs], dec, sr, h, mask)
        if backward:
            obr[:, vs] = o
        else:
            yr[:, vs] = _group_norm_gate(o + obr[:, vs], rng[:, vs], rg[:, vs].astype(F32)).astype(yr.dtype)


def _scan(p, ret_cos, ret_sin, wa2_pad, ba, n_ctx, backward, extras=None):
    b_, t_, _ = p.shape
    n_chunks = t_ // CHUNK
    n_cc = n_ctx // CHUNK
    d = D_MODEL

    if backward:
        def cidx(s):
            return jnp.where(s < n_cc, n_cc - 1 - s, n_chunks - 1 + n_cc - s)
    else:
        def cidx(s):
            return s

    def pspec(name, width):
        blk = COL[name] // width
        return pl.BlockSpec((None, CHUNK, width), lambda b, s: (b, cidx(s), blk))

    dirn = 1 if backward else 0
    tok = lambda width: pl.BlockSpec((None, CHUNK, width), lambda b, s: (b, cidx(s), 0))
    in_specs = [pspec('gla_q', 512), pspec('gla_k', 512), pspec('gla_v', 1024), pspec('gla_lr', LANES),
                pspec('ret_q', 512), pspec('ret_k', 512), pspec('ret_v', 1024),
                pl.BlockSpec((CHUNK, SCAN_DK), lambda b, s: (cidx(s), 0)),
                pl.BlockSpec((CHUNK, SCAN_DK), lambda b, s: (cidx(s), 0)),
                pl.BlockSpec((None, LANES, 512), lambda b, s: (dirn, 0, 0)),
                pl.BlockSpec((None, 1, 512), lambda b, s: (dirn, 0, 0))]
    args = [p, p, p, p, p, p, p, ret_cos, ret_sin, wa2_pad, ba]
    state = [pltpu.VMEM((SCAN_HEADS, SCAN_DV, SCAN_DK), F32), pltpu.VMEM((SCAN_HEADS, SCAN_DV, SCAN_DK), F32)]
    if backward:
        out_specs = [tok(d), tok(d)]
        out_shape = [jax.ShapeDtypeStruct((b_, t_, d), F32)] * 2
    else:
        obg, obr, gng, rng = extras
        in_specs += [pspec('gla_r', 1024), pspec('ret_g', 1024), tok(d), tok(d),
                     pl.BlockSpec((1, d), lambda b, s: (0, 0)), pl.BlockSpec((1, d), lambda b, s: (0, 0))]
        args += [p, p, obg, obr, gng.reshape(1, d), rng.reshape(1, d)]
        out_specs = [tok(d), tok(d)]
        out_shape = [jax.ShapeDtypeStruct((b_, t_, d), BF16)] * 2
    return pl.pallas_call(
        functools.partial(_scan_kernel, backward=backward),
        grid=(b_, n_chunks),
        in_specs=in_specs, out_specs=out_specs, out_shape=out_shape,
        scratch_shapes=state,
        compiler_params=_cparams(("arbitrary", "arbitrary")),
        name="scan_bwd" if backward else "scan_fwd",
    )(*args)


def _rope_axial(x, cosv, sinv, lane_lo):
    partner = jnp.where(lane_lo, pltpu.roll(x, 96, 1), pltpu.roll(x, 32, 1))
    return x * cosv + partner * sinv


def _attn_block(refs, local, first, last):
    (sink, q, kp, ks, kn, vp, vs, vn, kc, vc, cp, cs, cn, sp, ss, sn, o_ref) = refs
    qb_ = q.shape[0]
    hd = ATT_HEAD_DIM
    scale = hd ** -0.5
    rows = ATT_GROUP * qb_
    rowi = lax.broadcasted_iota(jnp.int32, (rows, 1), 0)
    if local:
        lane = lax.broadcasted_iota(jnp.int32, (qb_, hd), 1)
        lane_lo = (lane & 63) < 32
        i_idx = lax.broadcasted_iota(jnp.int32, (rows, qb_), 0) & (qb_ - 1)
        j_idx = lax.broadcasted_iota(jnp.int32, (rows, qb_), 1)
        mask_p = jnp.logical_and(j_idx >= i_idx, jnp.logical_not(first))
        mask_n = jnp.logical_and(j_idx <= i_idx, jnp.logical_not(last))

    for g in range(ATT_KV_HEADS):
        gs = slice(g * hd, (g + 1) * hd)
        heads = []
        for hh in range(ATT_GROUP):
            qs = slice((g * ATT_GROUP + hh) * hd, (g * ATT_GROUP + hh + 1) * hd)
            qh = q[:, qs].astype(F32)
            if local:
                qh = _rope_axial(qh, cs[...], ss[...], lane_lo)
            heads.append(qh.astype(BF16))
        q4 = jnp.concatenate(heads, axis=0)
        sink_col = jnp.full((rows, 1), sink[g * ATT_GROUP], F32)
        for hh in range(1, ATT_GROUP):
            sink_col = jnp.where(rowi >= hh * qb_, sink[g * ATT_GROUP + hh], sink_col)

        s_c = _dot_nt(q4, kc[:, gs]) * scale
        n_c = s_c.shape[1] // qb_
        m_el = s_c[:, 0:qb_]
        for cpart in range(1, n_c):
            m_el = jnp.maximum(m_el, s_c[:, cpart * qb_:(cpart + 1) * qb_])
        if local:
            k_p = _rope_axial(kp[:, gs].astype(F32), cp[...], sp[...], lane_lo).astype(BF16)
            k_s = _rope_axial(ks[:, gs].astype(F32), cs[...], ss[...], lane_lo).astype(BF16)
            k_n = _rope_axial(kn[:, gs].astype(F32), cn[...], sn[...], lane_lo).astype(BF16)
            s_p = jnp.where(mask_p, _dot_nt(q4, k_p) * scale, NEG)
            s_s = _dot_nt(q4, k_s) * scale
            s_n = jnp.where(mask_n, _dot_nt(q4, k_n) * scale, NEG)
            m_el = jnp.maximum(jnp.maximum(m_el, s_p), jnp.maximum(s_s, s_n))
        m = jnp.maximum(sink_col, jnp.max(m_el, axis=-1, keepdims=True))
        p_c = jnp.exp(s_c - m)
        d_el = p_c[:, 0:qb_]
        for cpart in range(1, n_c):
            d_el = d_el + p_c[:, cpart * qb_:(cpart + 1) * qb_]
        acc = _dot(p_c.astype(BF16), vc[:, gs])
        if local:
            p_p = jnp.exp(s_p - m)
            p_s = jnp.exp(s_s - m)
            p_n = jnp.exp(s_n - m)
            d_el = d_el + p_p + p_s + p_n
            acc = acc + _dot(p_p.astype(BF16), vp[:, gs]) + _dot(p_s.astype(BF16), vs[:, gs]) \
                + _dot(p_n.astype(BF16), vn[:, gs])
        den = jnp.exp(sink_col - m) + jnp.sum(d_el, axis=-1, keepdims=True)
        out = acc / den
        for hh in range(ATT_GROUP):
            qs = slice((g * ATT_GROUP + hh) * hd, (g * ATT_GROUP + hh + 1) * hd)
            o_ref[:, qs] = out[hh * qb_:(hh + 1) * qb_, :].astype(o_ref.dtype)


def _attn_kernel(*refs, n_cc, nb):
    n = pl.program_id(1)

    @pl.when(n < n_cc)
    def _():
        _attn_block(refs, False, None, None)

    @pl.when(n >= n_cc)
    def _():
        _attn_block(refs, True, n == n_cc, n == n_cc + nb - 1)


def _attention(p, sink, att_cos, att_sin, n_ctx):
    b_, t_, _ = p.shape
    qb_ = 128
    n_cc = n_ctx // qb_
    nb = (t_ - n_ctx) // qb_
    d = D_MODEL
    qblk, kblk, vblk = COL['att_q'] // 1024, COL['att_k'] // 256, COL['att_v'] // 256
    smem = pl.BlockSpec(memory_space=pltpu.SMEM)

    prev = lambda n: jnp.maximum(n - 1, n_cc)
    this = lambda n: jnp.maximum(n, n_cc)
    nxt = lambda n: jnp.maximum(jnp.minimum(n + 1, n_cc + nb - 1), n_cc)

    def tspec(width, blk, fn):
        return pl.BlockSpec((None, qb_, width), lambda b, n: (b, fn(n), blk))

    def tab(fn):
        return pl.BlockSpec((qb_, ATT_HEAD_DIM), lambda b, n: (fn(n) - n_cc, 0))

    ctx_k = pl.BlockSpec((None, n_ctx, 256), lambda b, n: (b, 0, kblk))
    ctx_v = pl.BlockSpec((None, n_ctx, 256), lambda b, n: (b, 0, vblk))
    return pl.pallas_call(
        functools.partial(_attn_kernel, n_cc=n_cc, nb=nb),
        grid=(b_, n_cc + nb),
        in_specs=[smem, tspec(1024, qblk, lambda n: n),
                  tspec(256, kblk, prev), tspec(256, kblk, this), tspec(256, kblk, nxt),
                  tspec(256, vblk, prev), tspec(256, vblk, this), tspec(256, vblk, nxt),
                  ctx_k, ctx_v, tab(prev), tab(this), tab(nxt), tab(prev), tab(this), tab(nxt)],
        out_specs=pl.BlockSpec((None, qb_, d), lambda b, n: (b, n, 0)),
        out_shape=jax.ShapeDtypeStruct((b_, t_, d), BF16),
        compiler_params=_cparams(("arbitrary", "arbitrary")),
        name="attention",
    )(sink, p, p, p, p, p, p, p, p, p, att_cos, att_cos, att_cos, att_sin, att_sin, att_sin)


def _merge_kernel(yg, ya, yr, pg, pa, pr, x_ref, mod_ref, g2_ref, wg, wa, wr, wo, wrh, wrl, br,
                  xo_ref, h2_ref, route_ref, cnt_ref, cnt_scr, *, tm):
    first = jnp.logical_and(pl.program_id(0) == 0, pl.program_id(1) == 0)

    @pl.when(first)
    def _():
        cnt_scr[...] = jnp.zeros_like(cnt_scr)

    merged = (_sigmoid(pg[...].astype(F32)) * _dot(yg[...], wg[...])
              + _sigmoid(pa[...].astype(F32)) * _dot(ya[...], wa[...])
              + _sigmoid(pr[...].astype(F32)) * _dot(yr[...], wr[...]))
    mix = _dot(merged.astype(BF16), wo[...])
    x = x_ref[...] + mod_ref[2:3, :] * mix
    xo_ref[...] = x
    y = x * lax.rsqrt(jnp.mean(x * x, axis=-1, keepdims=True) + NORM_EPS) * g2_ref[...]
    h2 = y * (1.0 + mod_ref[4:5, :]) + mod_ref[3:4, :]
    h2_ref[...] = h2

    h_hi = h2.astype(BF16)
    h_lo = (h2 - h_hi.astype(F32)).astype(BF16)
    logits = _dot(h_hi, wrh[...]) + _dot(h_lo, wrh[...]) + _dot(h_hi, wrl[...]) + br[...]

    lane = lax.broadcasted_iota(jnp.int32, (tm, LANES), 1)
    lanef = lane.astype(F32)
    is_grp = lane < MOE_GROUPS
    gl = jnp.where(is_grp, logits, NEG)
    gmax = jnp.max(gl, axis=-1, keepdims=True)
    gidx = jnp.min(jnp.where(gl == gmax, lanef, float(LANES)), axis=-1, keepdims=True)
    gprob = 1.0 / jnp.sum(jnp.where(is_grp, jnp.exp(gl - gmax), 0.0), axis=-1, keepdims=True)
    lo = EXPERT_LANE0 + MOE_EPG * gidx
    el = jnp.where(jnp.logical_and(lanef >= lo, lanef < lo + MOE_EPG), logits, NEG)
    t1 = jnp.max(el, axis=-1, keepdims=True)
    i1 = jnp.min(jnp.where(el == t1, lanef, float(LANES)), axis=-1, keepdims=True)
    el2 = jnp.where(lanef == i1, NEG, el)
    t2 = jnp.max(el2, axis=-1, keepdims=True)
    i2 = jnp.min(jnp.where(el2 == t2, lanef, float(LANES)), axis=-1, keepdims=True)
    ex = jnp.exp(t2 - t1)
    w0 = gprob / (1.0 + ex)
    w1 = gprob * ex / (1.0 + ex)

    oh0 = lanef == i1
    oh1 = lanef == i2
    both = jnp.where(jnp.logical_or(oh0, oh1), 1.0, 0.0)
    rr = lax.broadcasted_iota(jnp.int32, (tm, tm), 0)
    cc = lax.broadcasted_iota(jnp.int32, (tm, tm), 1)
    tri = jnp.where(cc < rr, 1.0, 0.0).astype(BF16)
    base = cnt_scr[...] + _dot(tri, both.astype(BF16))
    r0 = jnp.sum(jnp.where(oh0, base, 0.0), axis=-1, keepdims=True)
    r1 = jnp.sum(jnp.where(oh1, base, 0.0), axis=-1, keepdims=True)
    cnt = cnt_scr[...] + jnp.sum(both, axis=0, keepdims=True)
    cnt_scr[...] = cnt
    cnt_ref[...] = jnp.broadcast_to(cnt, cnt_ref.shape)

    out = jnp.where(lane == 0, i1 - EXPERT_LANE0, 0.0)
    out = jnp.where(lane == 1, i2 - EXPERT_LANE0, out)
    out = jnp.where(lane == 2, w0, out)
    out = jnp.where(lane == 3, w1, out)
    out = jnp.where(lane == 4, r0, out)
    out = jnp.where(lane == 5, r1, out)
    route_ref[...] = out


def _merge(yg, ya, yr, p, xs, mod, g2, wg, wa, wr, wo, wrh, wrl, br, n_ctx):
    b_, t_, d = xs.shape
    tm = 256
    nt = t_ // tm
    n_ct = n_ctx // tm
    gblk = COL['gates'] // d
    tok = pl.BlockSpec((None, tm, d), lambda b, t: (b, t, 0))
    wspec = pl.BlockSpec((d, d), lambda b, t: (0, 0))
    rspec = pl.BlockSpec((d, LANES), lambda b, t: (0, 0))

    def gate(i):
        return pl.BlockSpec((None, tm, d), lambda b, t: (b, t, gblk + i))

    return pl.pallas_call(
        functools.partial(_merge_kernel, tm=tm),
        grid=(b_, nt),
        in_specs=[tok, tok, tok, gate(0), gate(1), gate(2), tok,
                  pl.BlockSpec((None, None, 6, d), lambda b, t: (b, jnp.where(t < n_ct, 0, 1), 0, 0)),
                  pl.BlockSpec((1, d), lambda b, t: (0, 0)),
                  wspec, wspec, wspec, wspec, rspec, rspec,
                  pl.BlockSpec((1, LANES), lambda b, t: (0, 0))],
        out_specs=[tok, tok,
                   pl.BlockSpec((None, tm, LANES), lambda b, t: (b, t, 0)),
                   pl.BlockSpec((8, LANES), lambda b, t: (0, 0))],
        out_shape=[jax.ShapeDtypeStruct((b_, t_, d), F32), jax.ShapeDtypeStruct((b_, t_, d), F32),
                   jax.ShapeDtypeStruct((b_, t_, LANES), F32), jax.ShapeDtypeStruct((8, LANES), F32)],
        scratch_shapes=[pltpu.VMEM((1, LANES), F32)],
        compiler_params=_cparams(("arbitrary", "arbitrary")),
        name="merge_route",
    )(yg, ya, yr, p, p, p, xs, mod, g2.reshape(1, d), wg, wa, wr, wo, wrh, wrl, br)


ROW_DMA_UNROLL = 16


def _row_dmas(row_copy, tm):
    def issue(c, carry):
        for u in range(ROW_DMA_UNROLL):
            row_copy(c * ROW_DMA_UNROLL + u, 0).start()
            row_copy(c * ROW_DMA_UNROLL + u, 1).start()
        return carry

    lax.fori_loop(0, tm // ROW_DMA_UNROLL, issue, 0)

    def drain(c, carry):
        for u in range(ROW_DMA_UNROLL):
            row_copy(c * ROW_DMA_UNROLL + u, 0).wait()
            row_copy(c * ROW_DMA_UNROLL + u, 1).wait()
        return carry

    lax.fori_loop(0, tm // ROW_DMA_UNROLL, drain, 0)


def _dispatch_kernel(dest_ref, h_ref, init_ref, xs_ref, sem, *, tm):
    del init_ref

    def row_copy(i, k):
        return pltpu.make_async_copy(h_ref.at[pl.ds(i, 1)], xs_ref.at[pl.ds(dest_ref[0, k * tm + i], 1)], sem)

    _row_dmas(row_copy, tm)


def _dispatch(h2, dest, x_init):
    b_, t_, d = h2.shape
    n_slots = x_init.shape[0]
    tm = 256
    nt = t_ // tm
    return pl.pallas_call(
        functools.partial(_dispatch_kernel, tm=tm),
        grid=(b_, nt),
        in_specs=[pl.BlockSpec((None, 1, 2 * tm), lambda b, t: (b * nt + t, 0, 0), memory_space=pltpu.SMEM),
                  pl.BlockSpec((tm, d), lambda b, t: (b * nt + t, 0)),
                  pl.BlockSpec(memory_space=pl.ANY)],
        out_specs=pl.BlockSpec(memory_space=pl.ANY),
        out_shape=jax.ShapeDtypeStruct((n_slots, d), F32),
        input_output_aliases={2: 0},
        scratch_shapes=[pltpu.SemaphoreType.DMA],
        compiler_params=_cparams(("arbitrary", "arbitrary")),
        name="moe_dispatch",
    )(dest, h2.reshape(b_ * t_, d), x_init)


def _expert_kernel(be_ref, nu_ref, x_ref, w1_ref, w3_ref, w2_ref, y_ref, wb):
    i = pl.program_id(0)
    used = i < nu_ref[0]
    new_expert = jnp.logical_or(i == 0, be_ref[i] != be_ref[jnp.maximum(i - 1, 0)])

    @pl.when(jnp.logical_and(used, new_expert))
    def _():
        wb[0] = w1_ref[...].astype(BF16)
        wb[1] = w3_ref[...].astype(BF16)
        wb[2] = w2_ref[...].astype(BF16)

    @pl.when(used)
    def _():
        xb = x_ref[...].astype(BF16)
        h1 = _dot(xb, wb[0])
        h3 = _dot(xb, wb[1])
        y_ref[...] = _dot((_silu(h1) * h3).astype(BF16), wb[2])

    @pl.when(i >= nu_ref[0])
    def _():
        y_ref[...] = jnp.zeros_like(y_ref)


def _experts(x_sorted, blk_exp, n_used, w1, w3, w2, layer):
    n_slots, d = x_sorted.shape
    hid = w1.shape[-1]
    n_blocks = n_slots // MOE_BLOCK
    grid_spec = pltpu.PrefetchScalarGridSpec(
        num_scalar_prefetch=2,
        grid=(n_blocks,),
        in_specs=[pl.BlockSpec((MOE_BLOCK, d), lambda i, be, nu: (i, 0)),
                  pl.BlockSpec((None, None, d, hid), lambda i, be, nu: (layer, be[i], 0, 0)),
                  pl.BlockSpec((None, None, d, hid), lambda i, be, nu: (layer, be[i], 0, 0)),
                  pl.BlockSpec((None, None, hid, d), lambda i, be, nu: (layer, be[i], 0, 0))],
        out_specs=pl.BlockSpec((MOE_BLOCK, d), lambda i, be, nu: (i, 0)),
        scratch_shapes=[pltpu.VMEM((3, d, hid), BF16)],
    )
    assert d == hid
    return pl.pallas_call(
        _expert_kernel,
        grid_spec=grid_spec,
        out_shape=jax.ShapeDtypeStruct((n_slots, d), F32),
        compiler_params=_cparams(("arbitrary",)),
        name="moe_experts",
    )(blk_exp, n_used, x_sorted, w1, w3, w2)


def _combine_kernel(dest_ref, x_ref, route_ref, mod_ref, fg_ref, y_ref, o_ref, buf, sem, *, tm, final):
    def row_copy(i, k):
        return pltpu.make_async_copy(y_ref.at[pl.ds(dest_ref[0, k * tm + i], 1)], buf.at[k, pl.ds(i, 1)], sem)

    _row_dmas(row_copy, tm)

    r = route_ref[...]
    f = buf[0] * r[:, 2:3] + buf[1] * r[:, 3:4]
    x = x_ref[...] + mod_ref[5:6, :] * f
    if final:
        x = x * lax.rsqrt(jnp.mean(x * x, axis=-1, keepdims=True) + NORM_EPS) * fg_ref[...]
    o_ref[...] = x


def _combine(xs, route, mod, final_g, y_sorted, dest, n_ctx, final):
    b_, t_, d = xs.shape
    tm = 256
    nt = t_ // tm
    n_ct = n_ctx // tm
    skip = n_ct if final else 0
    tok = lambda w: pl.BlockSpec((None, tm, w), lambda b, t: (b, t + skip, 0))
    return pl.pallas_call(
        functools.partial(_combine_kernel, tm=tm, final=final),
        grid=(b_, nt - skip),
        in_specs=[pl.BlockSpec((None, 1, 2 * tm), lambda b, t: (b * nt + t + skip, 0, 0), memory_space=pltpu.SMEM),
                  tok(d), tok(LANES),
                  pl.BlockSpec((None, None, 6, d), lambda b, t: (b, jnp.where(t + skip < n_ct, 0, 1), 0, 0)),
                  pl.BlockSpec((1, d), lambda b, t: (0, 0)),
                  pl.BlockSpec(memory_space=pl.ANY)],
        out_specs=pl.BlockSpec((None, tm, d), lambda b, t: (b, t, 0)),
        out_shape=jax.ShapeDtypeStruct((b_, t_ - skip * tm, d), F32),
        scratch_shapes=[pltpu.VMEM((2, tm, d), F32), pltpu.SemaphoreType.DMA],
        compiler_params=_cparams(("arbitrary", "arbitrary")),
        name="moe_combine",
    )(dest, xs, route, mod, final_g.reshape(1, d), y_sorted)


def _slot_plan(route, counts, tm):
    b_, t_, _ = route.shape
    n_tok = b_ * t_
    cnt = counts[0, EXPERT_LANE0:EXPERT_LANE0 + MOE_EXPERTS].astype(jnp.int32)
    padded = (cnt + MOE_BLOCK - 1) // MOE_BLOCK * MOE_BLOCK
    pad_end = jnp.cumsum(padded)
    pad_start = pad_end - padded
    n_blocks = (n_tok * 2 + MOE_EXPERTS * (MOE_BLOCK - 1) + MOE_BLOCK - 1) // MOE_BLOCK
    n_used = (pad_end[-1] // MOE_BLOCK).astype(jnp.int32)
    blk = jnp.minimum(jnp.arange(n_blocks, dtype=jnp.int32), n_used - 1) * MOE_BLOCK
    blk_exp = jnp.sum((pad_end[None, :] <= blk[:, None]).astype(jnp.int32), axis=1)
    blk_exp = jnp.minimum(blk_exp, MOE_EXPERTS - 1).astype(jnp.int32)
    e = route[..., 0:2].astype(jnp.int32)
    rank = route[..., 4:6].astype(jnp.int32)
    eid = jnp.arange(MOE_EXPERTS, dtype=jnp.int32)
    dest = rank + jnp.sum(jnp.where(e[..., None] == eid, pad_start, 0), axis=-1)
    dest = dest.reshape(b_ * t_ // tm, tm, 2).transpose(0, 2, 1).reshape(b_ * t_ // tm, 1, 2 * tm)
    return dest, blk_exp, n_used.reshape(1), n_blocks * MOE_BLOCK


def _permute_w_in(w_in):
    parts = [w_in[:, _REF_OFF[name][0]:_REF_OFF[name][0] + _REF_OFF[name][1]] for name in _OUR_ORDER]
    width = sum(part.shape[1] for part in parts)
    parts.append(jnp.zeros((w_in.shape[0], N_P - width), w_in.dtype))
    return jnp.concatenate(parts, axis=1).astype(BF16)


def _rope_tables(n_ctx, l_):
    half = SCAN_DK // 2
    inv = ROPE_BASE ** (-jnp.arange(half, dtype=F32) / half)
    ang = jnp.arange(n_ctx + l_, dtype=F32)[:, None] * inv[None, :]
    ret_cos = jnp.concatenate([jnp.cos(ang), jnp.cos(ang)], axis=-1)
    ret_sin = jnp.concatenate([-jnp.sin(ang), jnp.sin(ang)], axis=-1)
    q = ATT_HEAD_DIM // 4
    inv = ROPE_BASE ** (-jnp.arange(q, dtype=F32) / q)
    tpos = jnp.arange(l_)
    a_r = (tpos // GRID_W).astype(F32)[:, None] * inv[None, :]
    a_c = (tpos % GRID_W).astype(F32)[:, None] * inv[None, :]
    att_cos = jnp.concatenate([jnp.cos(a_r), jnp.cos(a_r), jnp.cos(a_c), jnp.cos(a_c)], axis=-1)
    att_sin = jnp.concatenate([-jnp.sin(a_r), jnp.sin(a_r), -jnp.sin(a_c), jnp.sin(a_c)], axis=-1)
    return ret_cos, ret_sin, att_cos, att_sin


def kernel(x, c, ctx, c_ctx, w_ada, b_ada, norm1_g, norm2_g, w_in, gla_wa2, gla_ba, gla_norm_g, attn_sink,
           ret_norm_g, w_br_gla, w_br_attn, w_br_ret, w_out, moe_w_grp, moe_b_grp, moe_w_exp, moe_b_exp,
           moe_w1, moe_w3, moe_w2, final_g):
    b_, l_, d = x.shape
    n_ctx = ctx.shape[1]
    depth = w_ada.shape[0]
    assert d == D_MODEL and n_ctx % 256 == 0 and l_ % 256 == 0 and b_ <= 8

    xs = jnp.concatenate([ctx, x], axis=1)
    cc = jnp.zeros((16, d), F32).at[:b_].set(c).at[b_].set(c_ctx)
    mod_all = _modulation(cc, w_ada, b_ada)
    ret_cos, ret_sin, att_cos, att_sin = _rope_tables(n_ctx, l_)

    for layer in range(depth):
        last = layer == depth - 1
        m = mod_all[layer].reshape(16, 6, d)
        mod = jnp.stack([jnp.broadcast_to(m[b_], (b_, 6, d)), m[:b_]], axis=1)

        w_p = _permute_w_in(w_in[layer])
        wa2_pad = jnp.zeros((2, LANES, 512), F32)
        wa2_pad = wa2_pad.at[0, 0:GLA_LOW_RANK].set(gla_wa2[layer, 0])
        wa2_pad = wa2_pad.at[1, GLA_LOW_RANK:2 * GLA_LOW_RANK].set(gla_wa2[layer, 1]).astype(BF16)
        ba = gla_ba[layer].reshape(2, 1, 512)

        p = _in_projection(xs, mod, norm1_g[layer], w_p, n_ctx)
        obg, obr = _scan(p, ret_cos, ret_sin, wa2_pad, ba, n_ctx, True)
        yg, yr = _scan(p, ret_cos, ret_sin, wa2_pad, ba, n_ctx, False,
                       (obg, obr, gla_norm_g[layer], ret_norm_g[layer]))
        ya = _attention(p, attn_sink[layer], att_cos, att_sin, n_ctx)

        w_route = jnp.zeros((d, LANES), F32)
        w_route = w_route.at[:, 0:MOE_GROUPS].set(moe_w_grp[layer])
        w_route = w_route.at[:, EXPERT_LANE0:EXPERT_LANE0 + MOE_EXPERTS].set(moe_w_exp[layer])
        wr_hi = w_route.astype(BF16)
        wr_lo = (w_route - wr_hi.astype(F32)).astype(BF16)
        b_route = jnp.zeros((1, LANES), F32)
        b_route = b_route.at[0, 0:MOE_GROUPS].set(moe_b_grp[layer])
        b_route = b_route.at[0, EXPERT_LANE0:EXPERT_LANE0 + MOE_EXPERTS].set(moe_b_exp[layer])

        xs, h2, route, counts = _merge(
            yg, ya, yr, p, xs, mod, norm2_g[layer],
            w_br_gla[layer].astype(BF16), w_br_attn[layer].astype(BF16), w_br_ret[layer].astype(BF16),
            w_out[layer].astype(BF16), wr_hi, wr_lo, b_route, n_ctx)

        dest, blk_exp, n_used, n_slots = _slot_plan(route, counts, 256)
        x_init = jnp.zeros((n_slots, d), F32) if layer == 0 else x_sorted
        x_sorted = _dispatch(h2, dest, x_init)
        y_sorted = _experts(x_sorted, blk_exp, n_used, moe_w1, moe_w3, moe_w2, layer)
        xs = _combine(xs, route, mod, final_g, y_sorted, dest, n_ctx, last)

    return xs
```

```python
import functools
import math

import jax
import jax.numpy as jnp
from jax import lax
from jax.experimental import pallas as pl
from jax.experimental.pallas import tpu as pltpu

F32 = jnp.float32
BF16 = jnp.bfloat16

D_MODEL = 1024
GRID_W = 64
CHUNK = 128
NORM_EPS = 1e-6
ROPE_BASE = 10000.0

SCAN_HEADS = 4
SCAN_DK = 128
SCAN_DV = 256
GLA_LOW_RANK = 16
GLA_TAU = 16.0

ATT_HEAD_DIM = 128
ATT_Q_HEADS = 8
ATT_KV_HEADS = 2
ATT_GROUP = ATT_Q_HEADS // ATT_KV_HEADS

MOE_GROUPS = 4
MOE_EPG = 8
MOE_EXPERTS = 32
MOE_BLOCK = 512
EXPERT_LANE0 = 32

LANES = 128
INPROJ_TN = 2816
NEG = -1e30

_REF_LAYOUT = (
    ('gla_q', 512), ('gla_k', 512), ('gla_v', 1024), ('gla_r', 1024), ('gla_lr', 32),
    ('att_q', 1024), ('att_k', 256), ('att_v', 256),
    ('ret_q', 512), ('ret_k', 512), ('ret_v', 1024), ('ret_g', 1024), ('gates', 3072),
)
_OUR_ORDER = ('gla_v', 'gla_r', 'att_q', 'ret_v', 'ret_g', 'gates', 'gla_q', 'gla_k', 'ret_q', 'ret_k',
              'att_k', 'att_v', 'gla_lr')


def _layout():
    ref_off, start = {}, 0
    for name, width in _REF_LAYOUT:
        ref_off[name] = (start, width)
        start += width
    col, off = {}, 0
    for name in _OUR_ORDER:
        col[name] = off
        off += ref_off[name][1]
    n_p = (off + INPROJ_TN - 1) // INPROJ_TN * INPROJ_TN
    return ref_off, col, n_p


_REF_OFF, COL, N_P = _layout()
VMEM_LIMIT = 56 * 1024 * 1024


def _cparams(sem):
    return pltpu.CompilerParams(dimension_semantics=sem, vmem_limit_bytes=VMEM_LIMIT)


def _silu(x):
    return x / (1.0 + jnp.exp(-x))


def _sigmoid(x):
    return 1.0 / (1.0 + jnp.exp(-x))


def _dot(a, b):
    return jnp.dot(a, b, preferred_element_type=F32)


def _dot_nt(a, b):
    return lax.dot_general(a, b, (((1,), (1,)), ((), ())), preferred_element_type=F32)


def _dot_tn(a, b):
    return lax.dot_general(a, b, (((0,), (0,)), ((), ())), preferred_element_type=F32)


def _mod_kernel(c_ref, w_ref, b_ref, o_ref):
    s = _silu(c_ref[...]).astype(BF16)
    o_ref[...] = _dot(s, w_ref[...].astype(BF16)) + b_ref[...]


def _modulation(cc, w_ada, b_ada):
    depth, d, n6 = w_ada.shape
    rows = cc.shape[0]
    tn = 512
    return pl.pallas_call(
        _mod_kernel,
        grid=(depth, n6 // tn),
        in_specs=[pl.BlockSpec((rows, d), lambda l, j: (0, 0)),
                  pl.BlockSpec((None, d, tn), lambda l, j: (l, 0, j)),
                  pl.BlockSpec((None, 1, tn), lambda l, j: (l, 0, j))],
        out_specs=pl.BlockSpec((None, rows, tn), lambda l, j: (l, 0, j)),
        out_shape=jax.ShapeDtypeStruct((depth, rows, n6), F32),
        compiler_params=_cparams(("arbitrary", "arbitrary")),
        name="modulation",
    )(cc, w_ada, b_ada.reshape(depth, 1, n6))


def _inproj_kernel(x_ref, mod_ref, g_ref, w_ref, o_ref, h_scr, *, n_ctx, tm):
    t = pl.program_id(1)

    @pl.when(pl.program_id(2) == 0)
    def _():
        x = x_ref[...]
        y = x * lax.rsqrt(jnp.mean(x * x, axis=-1, keepdims=True) + NORM_EPS) * g_ref[...]
        row = t * tm + lax.broadcasted_iota(jnp.int32, (tm, 1), 0)
        is_ctx = row < n_ctx
        shift = jnp.where(is_ctx, mod_ref[0, 0:1, :], mod_ref[1, 0:1, :])
        scale = jnp.where(is_ctx, mod_ref[0, 1:2, :], mod_ref[1, 1:2, :])
        h_scr[...] = (y * (1.0 + scale) + shift).astype(BF16)

    o_ref[...] = _dot(h_scr[...], w_ref[...]).astype(o_ref.dtype)


def _in_projection(xs, mod, g, w_p, n_ctx):
    b_, t_, d = xs.shape
    tm = 768 if t_ % 768 == 0 else 256
    tn = INPROJ_TN
    return pl.pallas_call(
        functools.partial(_inproj_kernel, n_ctx=n_ctx, tm=tm),
        grid=(b_, t_ // tm, N_P // tn),
        in_specs=[pl.BlockSpec((None, tm, d), lambda b, t, j: (b, t, 0)),
                  pl.BlockSpec((None, 2, 6, d), lambda b, t, j: (b, 0, 0, 0)),
                  pl.BlockSpec((1, d), lambda b, t, j: (0, 0)),
                  pl.BlockSpec((d, tn), lambda b, t, j: (0, j))],
        out_specs=pl.BlockSpec((None, tm, tn), lambda b, t, j: (b, t, j)),
        out_shape=jax.ShapeDtypeStruct((b_, t_, N_P), BF16),
        scratch_shapes=[pltpu.VMEM((tm, d), BF16)],
        compiler_params=_cparams(("arbitrary", "arbitrary", "arbitrary")),
        name="in_projection",
    )(xs, mod, g.reshape(1, d), w_p)


def _chunk_update(q_in, k_in, k_out, v_bf, dec, s_ref, idx, mask):
    qb = q_in.astype(BF16)
    sc = jnp.where(mask, _dot_nt(qb, k_in.astype(BF16)), 0.0)
    st = s_ref[idx]
    o = _dot(sc.astype(BF16), v_bf) + _dot_nt(qb, st.astype(BF16))
    s_ref[idx] = st * dec + _dot_tn(v_bf, k_out.astype(BF16))
    return o


def _split3(x):
    hi = x.astype(BF16)
    r = x - hi.astype(F32)
    mid = r.astype(BF16)
    lo = (r - mid.astype(F32)).astype(BF16)
    return hi, mid, lo


def _group_norm_gate(o, g, r):
    mu = jnp.mean(o, axis=-1, keepdims=True)
    oc = o - mu
    var = jnp.mean(oc * oc, axis=-1, keepdims=True)
    return oc * lax.rsqrt(var + NORM_EPS) * g * _silu(r)


SCAN_BATCH_BLOCK = 2


def _scan_kernel(*refs, backward, bb_n):
    if backward:
        (gq, gk, gv, lr, rq, rk, rv, cos, sin, wa2, ba, obg, obr, sg, sr) = refs
    else:
        (gq, gk, gv, lr, rq, rk, rv, cos, sin, wa2, ba, gr, rg, obg, obr, gng, rng, yg, yr, sg, sr) = refs
    c_ = CHUNK

    @pl.when(pl.program_id(1) == 0)
    def _():
        sg[...] = jnp.zeros_like(sg)
        sr[...] = jnp.zeros_like(sr)

    row = lax.broadcasted_iota(jnp.int32, (c_, c_), 0)
    col = lax.broadcasted_iota(jnp.int32, (c_, c_), 1)
    if backward:
        tri = jnp.where(col >= row, 1.0, 0.0).astype(BF16)
        mask = col > row
    else:
        tri = jnp.where(col <= row, 1.0, 0.0).astype(BF16)
        mask = col <= row
    last = 0 if backward else c_ - 1

    pos = lax.broadcasted_iota(jnp.int32, (c_, SCAN_DK), 0)
    steps = ((c_ - pos) if backward else (pos + 1)).astype(F32)
    cosv = cos[...]
    sinv = sin[...]
    scale = SCAN_DK ** -0.5

    for bb in range(bb_n):
        for h in range(SCAN_HEADS):
            ks = slice(h * SCAN_DK, (h + 1) * SCAN_DK)
            vs = slice(h * SCAN_DV, (h + 1) * SCAN_DV)
            ld = math.log(1.0 - 2.0 ** (-5.0 - h))
            c = steps * ld
            q = rq[bb, :, ks].astype(F32)
            k = rk[bb, :, ks].astype(F32)
            q = q * cosv + pltpu.roll(q, SCAN_DK // 2, 1) * sinv
            k = (k * cosv + pltpu.roll(k, SCAN_DK // 2, 1) * sinv) * scale
            dec = jnp.full((1, SCAN_DK), math.exp(c_ * ld), F32)
            o = _chunk_update(q * jnp.exp(c), k * jnp.exp(-c), k * jnp.exp(c_ * ld - c), rv[bb, :, vs], dec,
                              sr, (bb, h), mask)
            if backward:
                obr[bb, :, vs] = o
            else:
                yr[bb, :, vs] = _group_norm_gate(o + obr[bb, :, vs], rng[:, vs],
                                                 rg[bb, :, vs].astype(F32)).astype(yr.dtype)

    for bb in range(bb_n):
        pre = _dot(lr[bb], wa2[...]) + ba[...]
        la = (jnp.minimum(pre, 0.0) - jnp.log1p(jnp.exp(-jnp.abs(pre)))) * (1.0 / GLA_TAU)
        hi, mid, lo = _split3(la)
        cum = _dot(tri, hi) + _dot(tri, mid) + _dot(tri, lo)
        tot = cum[last:last + 1, :]
        for h in range(SCAN_HEADS):
            ks = slice(h * SCAN_DK, (h + 1) * SCAN_DK)
            vs = slice(h * SCAN_DV, (h + 1) * SCAN_DV)
            c = cum[:, ks]
            th = tot[:, ks]
            q = gq[bb, :, ks].astype(F32) * scale
            k = gk[bb, :, ks].astype(F32)
            o = _chunk_update(q * jnp.exp(c), k * jnp.exp(-c), k * jnp.exp(th - c), gv[bb, :, vs], jnp.exp(th),
                              sg, (bb, h), mask)
            if backward:
                obg[bb, :, vs] = o
            else:
                yg[bb, :, vs] = _group_norm_gate(o + obg[bb, :, vs], gng[:, vs],
                                                 gr[bb, :, vs].astype(F32)).astype(yg.dtype)


def _scan(p, ret_cos, ret_sin, wa2_pad, ba, n_ctx, backward, extras=None):
    b_, t_, _ = p.shape
    n_chunks = t_ // CHUNK
    n_cc = n_ctx // CHUNK
    d = D_MODEL
    bb_n = SCAN_BATCH_BLOCK if b_ % SCAN_BATCH_BLOCK == 0 else 1

    if backward:
        def cidx(s):
            return jnp.where(s < n_cc, n_cc - 1 - s, n_chunks - 1 + n_cc - s)
    else:
        def cidx(s):
            return s

    def pspec(name, width):
        blk = COL[name] // width
        return pl.BlockSpec((bb_n, CHUNK, width), lambda b, s: (b, cidx(s), blk))

    dirn = 1 if backward else 0
    tok = lambda width: pl.BlockSpec((bb_n, CHUNK, width), lambda b, s: (b, cidx(s), 0))
    in_specs = [pspec('gla_q', 512), pspec('gla_k', 512), pspec('gla_v', 1024), pspec('gla_lr', LANES),
                pspec('ret_q', 512), pspec('ret_k', 512), pspec('ret_v', 1024),
                pl.BlockSpec((CHUNK, SCAN_DK), lambda b, s: (cidx(s), 0)),
                pl.BlockSpec((CHUNK, SCAN_DK), lambda b, s: (cidx(s), 0)),
                pl.BlockSpec((None, LANES, 512), lambda b, s: (dirn, 0, 0)),
                pl.BlockSpec((None, 1, 512), lambda b, s: (dirn, 0, 0))]
    args = [p, p, p, p, p, p, p, ret_cos, ret_sin, wa2_pad, ba]
    state = [pltpu.VMEM((bb_n, SCAN_HEADS, SCAN_DV, SCAN_DK), F32),
             pltpu.VMEM((bb_n, SCAN_HEADS, SCAN_DV, SCAN_DK), F32)]
    if backward:
        out_specs = [tok(d), tok(d)]
        out_shape = [jax.ShapeDtypeStruct((b_, t_, d), F32)] * 2
    else:
        obg, obr, gng, rng = extras
        in_specs += [pspec('gla_r', 1024), pspec('ret_g', 1024), tok(d), tok(d),
                     pl.BlockSpec((1, d), lambda b, s: (0, 0)), pl.BlockSpec((1, d), lambda b, s: (0, 0))]
        args += [p, p, obg, obr, gng.reshape(1, d), rng.reshape(1, d)]
        out_specs = [tok(d), tok(d)]
        out_shape = [jax.ShapeDtypeStruct((b_, t_, d), BF16)] * 2
    return pl.pallas_call(
        functools.partial(_scan_kernel, backward=backward, bb_n=bb_n),
        grid=(b_ // bb_n, n_chunks),
        in_specs=in_specs, out_specs=out_specs, out_shape=out_shape,
        scratch_shapes=state,
        compiler_params=_cparams(("arbitrary", "arbitrary")),
        name="scan_bwd" if backward else "scan_fwd",
    )(*args)


def _rope_axial(x, cosv, sinv, lane_lo):
    partner = jnp.where(lane_lo, pltpu.roll(x, 96, 1), pltpu.roll(x, 32, 1))
    return x * cosv + partner * sinv


def _attn_block(refs, bb, local, first, last):
    (sink, q, kp, ks, kn, vp, vs, vn, kc, vc, cp, cs, cn, sp, ss, sn, o_ref) = refs
    qb_ = q.shape[1]
    hd = ATT_HEAD_DIM
    scale = hd ** -0.5
    rows = ATT_GROUP * qb_
    rowi = lax.broadcasted_iota(jnp.int32, (rows, 1), 0)
    if local:
        lane = lax.broadcasted_iota(jnp.int32, (qb_, hd), 1)
        lane_lo = (lane & 63) < 32
        i_idx = lax.broadcasted_iota(jnp.int32, (rows, qb_), 0) & (qb_ - 1)
        j_idx = lax.broadcasted_iota(jnp.int32, (rows, qb_), 1)
        mask_p = jnp.logical_and(j_idx >= i_idx, jnp.logical_not(first))
        mask_n = jnp.logical_and(j_idx <= i_idx, jnp.logical_not(last))

    for g in range(ATT_KV_HEADS):
        gs = slice(g * hd, (g + 1) * hd)
        heads = []
        for hh in range(ATT_GROUP):
            qs = slice((g * ATT_GROUP + hh) * hd, (g * ATT_GROUP + hh + 1) * hd)
            qh = q[bb, :, qs].astype(F32)
            if local:
                qh = _rope_axial(qh, cs[...], ss[...], lane_lo)
            heads.append(qh.astype(BF16))
        q4 = jnp.concatenate(heads, axis=0)
        sink_col = jnp.full((rows, 1), sink[g * ATT_GROUP], F32)
        for hh in range(1, ATT_GROUP):
            sink_col = jnp.where(rowi >= hh * qb_, sink[g * ATT_GROUP + hh], sink_col)

        s_c = _dot_nt(q4, kc[bb, :, gs]) * scale
        n_c = s_c.shape[1] // qb_
        m_el = s_c[:, 0:qb_]
        for cpart in range(1, n_c):
            m_el = jnp.maximum(m_el, s_c[:, cpart * qb_:(cpart + 1) * qb_])
        if local:
            k_p = _rope_axial(kp[bb, :, gs].astype(F32), cp[...], sp[...], lane_lo).astype(BF16)
            k_s = _rope_axial(ks[bb, :, gs].astype(F32), cs[...], ss[...], lane_lo).astype(BF16)
            k_n = _rope_axial(kn[bb, :, gs].astype(F32), cn[...], sn[...], lane_lo).astype(BF16)
            s_p = jnp.where(mask_p, _dot_nt(q4, k_p) * scale, NEG)
            s_s = _dot_nt(q4, k_s) * scale
            s_n = jnp.where(mask_n, _dot_nt(q4, k_n) * scale, NEG)
            m_el = jnp.maximum(jnp.maximum(m_el, s_p), jnp.maximum(s_s, s_n))
        m = jnp.maximum(sink_col, jnp.max(m_el, axis=-1, keepdims=True))
        p_c = jnp.exp(s_c - m)
        d_el = p_c[:, 0:qb_]
        for cpart in range(1, n_c):
            d_el = d_el + p_c[:, cpart * qb_:(cpart + 1) * qb_]
        acc = _dot(p_c.astype(BF16), vc[bb, :, gs])
        if local:
            p_p = jnp.exp(s_p - m)
            p_s = jnp.exp(s_s - m)
            p_n = jnp.exp(s_n - m)
            d_el = d_el + p_p + p_s + p_n
            acc = acc + _dot(p_p.astype(BF16), vp[bb, :, gs]) + _dot(p_s.astype(BF16), vs[bb, :, gs]) \
                + _dot(p_n.astype(BF16), vn[bb, :, gs])
        den = jnp.exp(sink_col - m) + jnp.sum(d_el, axis=-1, keepdims=True)
        out = acc / den
        for hh in range(ATT_GROUP):
            qs = slice((g * ATT_GROUP + hh) * hd, (g * ATT_GROUP + hh + 1) * hd)
            o_ref[bb, :, qs] = out[hh * qb_:(hh + 1) * qb_, :].astype(o_ref.dtype)


ATTN_BATCH_BLOCK = 2


def _attn_kernel(*refs, n_cc, nb, bb_n):
    n = pl.program_id(1)

    @pl.when(n < n_cc)
    def _():
        for bb in range(bb_n):
            _attn_block(refs, bb, False, None, None)

    @pl.when(n >= n_cc)
    def _():
        for bb in range(bb_n):
            _attn_block(refs, bb, True, n == n_cc, n == n_cc + nb - 1)


def _attention(p, sink, att_cos, att_sin, n_ctx):
    b_, t_, _ = p.shape
    qb_ = 128
    n_cc = n_ctx // qb_
    nb = (t_ - n_ctx) // qb_
    d = D_MODEL
    bb_n = ATTN_BATCH_BLOCK if b_ % ATTN_BATCH_BLOCK == 0 else 1
    qblk, kblk, vblk = COL['att_q'] // 1024, COL['att_k'] // 256, COL['att_v'] // 256
    smem = pl.BlockSpec(memory_space=pltpu.SMEM)

    prev = lambda n: jnp.maximum(n - 1, n_cc)
    this = lambda n: jnp.maximum(n, n_cc)
    nxt = lambda n: jnp.maximum(jnp.minimum(n + 1, n_cc + nb - 1), n_cc)

    def tspec(width, blk, fn):
        return pl.BlockSpec((bb_n, qb_, width), lambda b, n: (b, fn(n), blk))

    def tab(fn):
        return pl.BlockSpec((qb_, ATT_HEAD_DIM), lambda b, n: (fn(n) - n_cc, 0))

    ctx_k = pl.BlockSpec((bb_n, n_ctx, 256), lambda b, n: (b, 0, kblk))
    ctx_v = pl.BlockSpec((bb_n, n_ctx, 256), lambda b, n: (b, 0, vblk))
    return pl.pallas_call(
        functools.partial(_attn_kernel, n_cc=n_cc, nb=nb, bb_n=bb_n),
        grid=(b_ // bb_n, n_cc + nb),
        in_specs=[smem, tspec(1024, qblk, lambda n: n),
                  tspec(256, kblk, prev), tspec(256, kblk, this), tspec(256, kblk, nxt),
                  tspec(256, vblk, prev), tspec(256, vblk, this), tspec(256, vblk, nxt),
                  ctx_k, ctx_v, tab(prev), tab(this), tab(nxt), tab(prev), tab(this), tab(nxt)],
        out_specs=pl.BlockSpec((bb_n, qb_, d), lambda b, n: (b, n, 0)),
        out_shape=jax.ShapeDtypeStruct((b_, t_, d), BF16),
        compiler_params=_cparams(("arbitrary", "arbitrary")),
        name="attention",
    )(sink, p, p, p, p, p, p, p, p, p, att_cos, att_cos, att_cos, att_sin, att_sin, att_sin)


def _merge_kernel(yg, ya, yr, pg, pa, pr, x_ref, mod_ref, g2_ref, wg, wa, wr, wo, wrh, wrl, br,
                  xo_ref, h2_ref, route_ref, cnt_ref, cnt_scr, *, tm):
    first = jnp.logical_and(pl.program_id(0) == 0, pl.program_id(1) == 0)

    @pl.when(first)
    def _():
        cnt_scr[...] = jnp.zeros_like(cnt_scr)

    merged = (_sigmoid(pg[...].astype(F32)) * _dot(yg[...], wg[...])
              + _sigmoid(pa[...].astype(F32)) * _dot(ya[...], wa[...])
              + _sigmoid(pr[...].astype(F32)) * _dot(yr[...], wr[...]))
    mix = _dot(merged.astype(BF16), wo[...])
    x = x_ref[...] + mod_ref[2:3, :] * mix
    xo_ref[...] = x
    y = x * lax.rsqrt(jnp.mean(x * x, axis=-1, keepdims=True) + NORM_EPS) * g2_ref[...]
    h2 = y * (1.0 + mod_ref[4:5, :]) + mod_ref[3:4, :]
    h2_ref[...] = h2

    h_hi = h2.astype(BF16)
    h_lo = (h2 - h_hi.astype(F32)).astype(BF16)
    logits = _dot(h_hi, wrh[...]) + _dot(h_lo, wrh[...]) + _dot(h_hi, wrl[...]) + br[...]

    lane = lax.broadcasted_iota(jnp.int32, (tm, LANES), 1)
    lanef = lane.astype(F32)
    is_grp = lane < MOE_GROUPS
    gl = jnp.where(is_grp, logits, NEG)
    gmax = jnp.max(gl, axis=-1, keepdims=True)
    gidx = jnp.min(jnp.where(gl == gmax, lanef, float(LANES)), axis=-1, keepdims=True)
    gprob = 1.0 / jnp.sum(jnp.where(is_grp, jnp.exp(gl - gmax), 0.0), axis=-1, keepdims=True)
    lo = EXPERT_LANE0 + MOE_EPG * gidx
    el = jnp.where(jnp.logical_and(lanef >= lo, lanef < lo + MOE_EPG), logits, NEG)
    t1 = jnp.max(el, axis=-1, keepdims=True)
    i1 = jnp.min(jnp.where(el == t1, lanef, float(LANES)), axis=-1, keepdims=True)
    el2 = jnp.where(lanef == i1, NEG, el)
    t2 = jnp.max(el2, axis=-1, keepdims=True)
    i2 = jnp.min(jnp.where(el2 == t2, lanef, float(LANES)), axis=-1, keepdims=True)
    ex = jnp.exp(t2 - t1)
    w0 = gprob / (1.0 + ex)
    w1 = gprob * ex / (1.0 + ex)

    oh0 = lanef == i1
    oh1 = lanef == i2
    both = jnp.where(jnp.logical_or(oh0, oh1), 1.0, 0.0)
    rr = lax.broadcasted_iota(jnp.int32, (tm, tm), 0)
    cc = lax.broadcasted_iota(jnp.int32, (tm, tm), 1)
    tri = jnp.where(cc < rr, 1.0, 0.0).astype(BF16)
    base = cnt_scr[...] + _dot(tri, both.astype(BF16))
    r0 = jnp.sum(jnp.where(oh0, base, 0.0), axis=-1, keepdims=True)
    r1 = jnp.sum(jnp.where(oh1, base, 0.0), axis=-1, keepdims=True)
    cnt = cnt_scr[...] + jnp.sum(both, axis=0, keepdims=True)
    cnt_scr[...] = cnt
    cnt_ref[...] = jnp.broadcast_to(cnt, cnt_ref.shape)

    out = jnp.where(lane == 0, i1 - EXPERT_LANE0, 0.0)
    out = jnp.where(lane == 1, i2 - EXPERT_LANE0, out)
    out = jnp.where(lane == 2, w0, out)
    out = jnp.where(lane == 3, w1, out)
    out = jnp.where(lane == 4, r0, out)
    out = jnp.where(lane == 5, r1, out)
    route_ref[...] = out


def _merge(yg, ya, yr, p, xs, mod, g2, wg, wa, wr, wo, wrh, wrl, br, n_ctx):
    b_, t_, d = xs.shape
    tm = 256
    nt = t_ // tm
    n_ct = n_ctx // tm
    gblk = COL['gates'] // d
    tok = pl.BlockSpec((None, tm, d), lambda b, t: (b, t, 0))
    wspec = pl.BlockSpec((d, d), lambda b, t: (0, 0))
    rspec = pl.BlockSpec((d, LANES), lambda b, t: (0, 0))

    def gate(i):
        return pl.BlockSpec((None, tm, d), lambda b, t: (b, t, gblk + i))

    return pl.pallas_call(
        functools.partial(_merge_kernel, tm=tm),
        grid=(b_, nt),
        in_specs=[tok, tok, tok, gate(0), gate(1), gate(2), tok,
                  pl.BlockSpec((None, None, 6, d), lambda b, t: (b, jnp.where(t < n_ct, 0, 1), 0, 0)),
                  pl.BlockSpec((1, d), lambda b, t: (0, 0)),
                  wspec, wspec, wspec, wspec, rspec, rspec,
                  pl.BlockSpec((1, LANES), lambda b, t: (0, 0))],
        out_specs=[tok, tok,
                   pl.BlockSpec((None, tm, LANES), lambda b, t: (b, t, 0)),
                   pl.BlockSpec((8, LANES), lambda b, t: (0, 0))],
        out_shape=[jax.ShapeDtypeStruct((b_, t_, d), F32), jax.ShapeDtypeStruct((b_, t_, d), F32),
                   jax.ShapeDtypeStruct((b_, t_, LANES), F32), jax.ShapeDtypeStruct((8, LANES), F32)],
        scratch_shapes=[pltpu.VMEM((1, LANES), F32)],
        compiler_params=_cparams(("arbitrary", "arbitrary")),
        name="merge_route",
    )(yg, ya, yr, p, p, p, xs, mod, g2.reshape(1, d), wg, wa, wr, wo, wrh, wrl, br)


ROW_DMA_UNROLL = 16


def _issue_rows(row_copy, tm):
    def issue(c, carry):
        for u in range(ROW_DMA_UNROLL):
            row_copy(c * ROW_DMA_UNROLL + u, 0).start()
            row_copy(c * ROW_DMA_UNROLL + u, 1).start()
        return carry

    lax.fori_loop(0, tm // ROW_DMA_UNROLL, issue, 0)


def _drain_rows(row_copy, tm):
    def drain(c, carry):
        for u in range(ROW_DMA_UNROLL):
            row_copy(c * ROW_DMA_UNROLL + u, 0).wait()
            row_copy(c * ROW_DMA_UNROLL + u, 1).wait()
        return carry

    lax.fori_loop(0, tm // ROW_DMA_UNROLL, drain, 0)


def _dispatch_kernel(dest_ref, h_ref, init_ref, xs_ref, sem, *, tm):
    del init_ref

    def row_copy(i, k):
        return pltpu.make_async_copy(h_ref.at[pl.ds(i, 1)], xs_ref.at[pl.ds(dest_ref[0, k * tm + i], 1)], sem)

    _issue_rows(row_copy, tm)
    _drain_rows(row_copy, tm)


def _dispatch(h2, dest, x_init):
    b_, t_, d = h2.shape
    n_slots = x_init.shape[0]
    tm = 256
    nt = t_ // tm
    return pl.pallas_call(
        functools.partial(_dispatch_kernel, tm=tm),
        grid=(b_, nt),
        in_specs=[pl.BlockSpec((None, 1, 2 * tm), lambda b, t: (b * nt + t, 0, 0), memory_space=pltpu.SMEM),
                  pl.BlockSpec((tm, d), lambda b, t: (b * nt + t, 0)),
                  pl.BlockSpec(memory_space=pl.ANY)],
        out_specs=pl.BlockSpec(memory_space=pl.ANY),
        out_shape=jax.ShapeDtypeStruct((n_slots, d), F32),
        input_output_aliases={2: 0},
        scratch_shapes=[pltpu.SemaphoreType.DMA],
        compiler_params=_cparams(("arbitrary", "arbitrary")),
        name="moe_dispatch",
    )(dest, h2.reshape(b_ * t_, d), x_init)


def _expert_kernel(be_ref, nu_ref, x_ref, w1_ref, w3_ref, w2_ref, y_ref, wb):
    i = pl.program_id(0)
    used = i < nu_ref[0]
    new_expert = jnp.logical_or(i == 0, be_ref[i] != be_ref[jnp.maximum(i - 1, 0)])

    @pl.when(jnp.logical_and(used, new_expert))
    def _():
        wb[0] = w1_ref[...].astype(BF16)
        wb[1] = w3_ref[...].astype(BF16)
        wb[2] = w2_ref[...].astype(BF16)

    @pl.when(used)
    def _():
        xb = x_ref[...].astype(BF16)
        h1 = _dot(xb, wb[0])
        h3 = _dot(xb, wb[1])
        y_ref[...] = _dot((_silu(h1) * h3).astype(BF16), wb[2])

    @pl.when(i >= nu_ref[0])
    def _():
        y_ref[...] = jnp.zeros_like(y_ref)


def _experts(x_sorted, blk_exp, n_used, w1, w3, w2, layer):
    n_slots, d = x_sorted.shape
    hid = w1.shape[-1]
    n_blocks = n_slots // MOE_BLOCK
    grid_spec = pltpu.PrefetchScalarGridSpec(
        num_scalar_prefetch=2,
        grid=(n_blocks,),
        in_specs=[pl.BlockSpec((MOE_BLOCK, d), lambda i, be, nu: (i, 0)),
                  pl.BlockSpec((None, None, d, hid), lambda i, be, nu: (layer, be[i], 0, 0)),
                  pl.BlockSpec((None, None, d, hid), lambda i, be, nu: (layer, be[i], 0, 0)),
                  pl.BlockSpec((None, None, hid, d), lambda i, be, nu: (layer, be[i], 0, 0))],
        out_specs=pl.BlockSpec((MOE_BLOCK, d), lambda i, be, nu: (i, 0)),
        scratch_shapes=[pltpu.VMEM((3, d, hid), BF16)],
    )
    assert d == hid
    return pl.pallas_call(
        _expert_kernel,
        grid_spec=grid_spec,
        out_shape=jax.ShapeDtypeStruct((n_slots, d), F32),
        compiler_params=_cparams(("arbitrary",)),
        name="moe_experts",
    )(blk_exp, n_used, x_sorted, w1, w3, w2)


def _combine_kernel(dcur_ref, dnext_ref, x_ref, route_ref, mod_ref, fg_ref, y_ref, o_ref, buf, sem, *,
                    tm, final, n_steps):
    g = pl.program_id(0)
    slot = g % 2

    def row_copy(dest_ref, sl):
        def make(i, k):
            return pltpu.make_async_copy(y_ref.at[pl.ds(dest_ref[0, k * tm + i], 1)],
                                         buf.at[sl, k, pl.ds(i, 1)], sem.at[sl])
        return make

    @pl.when(g == 0)
    def _():
        _issue_rows(row_copy(dcur_ref, 0), tm)

    @pl.when(g + 1 < n_steps)
    def _():
        _issue_rows(row_copy(dnext_ref, 1 - slot), tm)

    _drain_rows(row_copy(dcur_ref, slot), tm)

    r = route_ref[...]
    f = buf[slot, 0] * r[:, 2:3] + buf[slot, 1] * r[:, 3:4]
    x = x_ref[...] + mod_ref[5:6, :] * f
    if final:
        x = x * lax.rsqrt(jnp.mean(x * x, axis=-1, keepdims=True) + NORM_EPS) * fg_ref[...]
    o_ref[...] = x


def _combine(xs, route, mod, final_g, y_sorted, dest, n_ctx, final):
    b_, t_, d = xs.shape
    tm = 256
    nt = t_ // tm
    n_ct = n_ctx // tm
    skip = n_ct if final else 0
    ntl = nt - skip
    n_steps = b_ * ntl

    def bt(g):
        return g // ntl, g % ntl + skip

    def tok(w):
        return pl.BlockSpec((None, tm, w), lambda g: (*bt(g), 0))

    def dspec(off):
        def imap(g):
            b, t = bt(jnp.minimum(g + off, n_steps - 1))
            return b * nt + t, 0, 0
        return pl.BlockSpec((None, 1, 2 * tm), imap, memory_space=pltpu.SMEM)

    return pl.pallas_call(
        functools.partial(_combine_kernel, tm=tm, final=final, n_steps=n_steps),
        grid=(n_steps,),
        in_specs=[dspec(0), dspec(1), tok(d), tok(LANES),
                  pl.BlockSpec((None, None, 6, d), lambda g: (g // ntl, jnp.where(g % ntl + skip < n_ct, 0, 1), 0, 0)),
                  pl.BlockSpec((1, d), lambda g: (0, 0)),
                  pl.BlockSpec(memory_space=pl.ANY)],
        out_specs=pl.BlockSpec((None, tm, d), lambda g: (g // ntl, g % ntl, 0)),
        out_shape=jax.ShapeDtypeStruct((b_, t_ - skip * tm, d), F32),
        scratch_shapes=[pltpu.VMEM((2, 2, tm, d), F32), pltpu.SemaphoreType.DMA((2,))],
        compiler_params=_cparams(("arbitrary",)),
        name="moe_combine",
    )(dest, dest, xs, route, mod, final_g.reshape(1, d), y_sorted)


def _slot_plan(route, counts, tm):
    b_, t_, _ = route.shape
    n_tok = b_ * t_
    cnt = counts[0, EXPERT_LANE0:EXPERT_LANE0 + MOE_EXPERTS].astype(jnp.int32)
    padded = (cnt + MOE_BLOCK - 1) // MOE_BLOCK * MOE_BLOCK
    pad_end = jnp.cumsum(padded)
    pad_start = pad_end - padded
    n_blocks = (n_tok * 2 + MOE_EXPERTS * (MOE_BLOCK - 1) + MOE_BLOCK - 1) // MOE_BLOCK
    n_used = (pad_end[-1] // MOE_BLOCK).astype(jnp.int32)
    blk = jnp.minimum(jnp.arange(n_blocks, dtype=jnp.int32), n_used - 1) * MOE_BLOCK
    blk_exp = jnp.sum((pad_end[None, :] <= blk[:, None]).astype(jnp.int32), axis=1)
    blk_exp = jnp.minimum(blk_exp, MOE_EXPERTS - 1).astype(jnp.int32)
    e = route[..., 0:2].astype(jnp.int32)
    rank = route[..., 4:6].astype(jnp.int32)
    eid = jnp.arange(MOE_EXPERTS, dtype=jnp.int32)
    dest = rank + jnp.sum(jnp.where(e[..., None] == eid, pad_start, 0), axis=-1)
    dest = dest.reshape(b_ * t_ // tm, tm, 2).transpose(0, 2, 1).reshape(b_ * t_ // tm, 1, 2 * tm)
    return dest, blk_exp, n_used.reshape(1), n_blocks * MOE_BLOCK


def _permute_w_in(w_in):
    parts = [w_in[:, _REF_OFF[name][0]:_REF_OFF[name][0] + _REF_OFF[name][1]] for name in _OUR_ORDER]
    width = sum(part.shape[1] for part in parts)
    parts.append(jnp.zeros((w_in.shape[0], N_P - width), w_in.dtype))
    return jnp.concatenate(parts, axis=1).astype(BF16)


def _rope_tables(n_ctx, l_):
    half = SCAN_DK // 2
    inv = ROPE_BASE ** (-jnp.arange(half, dtype=F32) / half)
    ang = jnp.arange(n_ctx + l_, dtype=F32)[:, None] * inv[None, :]
    ret_cos = jnp.concatenate([jnp.cos(ang), jnp.cos(ang)], axis=-1)
    ret_sin = jnp.concatenate([-jnp.sin(ang), jnp.sin(ang)], axis=-1)
    q = ATT_HEAD_DIM // 4
    inv = ROPE_BASE ** (-jnp.arange(q, dtype=F32) / q)
    tpos = jnp.arange(l_)
    a_r = (tpos // GRID_W).astype(F32)[:, None] * inv[None, :]
    a_c = (tpos % GRID_W).astype(F32)[:, None] * inv[None, :]
    att_cos = jnp.concatenate([jnp.cos(a_r), jnp.cos(a_r), jnp.cos(a_c), jnp.cos(a_c)], axis=-1)
    att_sin = jnp.concatenate([-jnp.sin(a_r), jnp.sin(a_r), -jnp.sin(a_c), jnp.sin(a_c)], axis=-1)
    return ret_cos, ret_sin, att_cos, att_sin


def kernel(x, c, ctx, c_ctx, w_ada, b_ada, norm1_g, norm2_g, w_in, gla_wa2, gla_ba, gla_norm_g, attn_sink,
           ret_norm_g, w_br_gla, w_br_attn, w_br_ret, w_out, moe_w_grp, moe_b_grp, moe_w_exp, moe_b_exp,
           moe_w1, moe_w3, moe_w2, final_g):
    b_, l_, d = x.shape
    n_ctx = ctx.shape[1]
    depth = w_ada.shape[0]
    assert d == D_MODEL and n_ctx % 256 == 0 and l_ % 256 == 0 and b_ <= 8

    xs = jnp.concatenate([ctx, x], axis=1)
    cc = jnp.zeros((16, d), F32).at[:b_].set(c).at[b_].set(c_ctx)
    mod_all = _modulation(cc, w_ada, b_ada)
    ret_cos, ret_sin, att_cos, att_sin = _rope_tables(n_ctx, l_)

    for layer in range(depth):
        last = layer == depth - 1
        m = mod_all[layer].reshape(16, 6, d)
        mod = jnp.stack([jnp.broadcast_to(m[b_], (b_, 6, d)), m[:b_]], axis=1)

        w_p = _permute_w_in(w_in[layer])
        wa2_pad = jnp.zeros((2, LANES, 512), F32)
        wa2_pad = wa2_pad.at[0, 0:GLA_LOW_RANK].set(gla_wa2[layer, 0])
        wa2_pad = wa2_pad.at[1, GLA_LOW_RANK:2 * GLA_LOW_RANK].set(gla_wa2[layer, 1]).astype(BF16)
        ba = gla_ba[layer].reshape(2, 1, 512)

        p = _in_projection(xs, mod, norm1_g[layer], w_p, n_ctx)
        obg, obr = _scan(p, ret_cos, ret_sin, wa2_pad, ba, n_ctx, True)
        yg, yr = _scan(p, ret_cos, ret_sin, wa2_pad, ba, n_ctx, False,
                       (obg, obr, gla_norm_g[layer], ret_norm_g[layer]))
        ya = _attention(p, attn_sink[layer], att_cos, att_sin, n_ctx)

        w_route = jnp.zeros((d, LANES), F32)
        w_route = w_route.at[:, 0:MOE_GROUPS].set(moe_w_grp[layer])
        w_route = w_route.at[:, EXPERT_LANE0:EXPERT_LANE0 + MOE_EXPERTS].set(moe_w_exp[layer])
        wr_hi = w_route.astype(BF16)
        wr_lo = (w_route - wr_hi.astype(F32)).astype(BF16)
        b_route = jnp.zeros((1, LANES), F32)
        b_route = b_route.at[0, 0:MOE_GROUPS].set(moe_b_grp[layer])
        b_route = b_route.at[0, EXPERT_LANE0:EXPERT_LANE0 + MOE_EXPERTS].set(moe_b_exp[layer])

        xs, h2, route, counts = _merge(
            yg, ya, yr, p, xs, mod, norm2_g[layer],
            w_br_gla[layer].astype(BF16), w_br_attn[layer].astype(BF16), w_br_ret[layer].astype(BF16),
            w_out[layer].astype(BF16), wr_hi, wr_lo, b_route, n_ctx)

        dest, blk_exp, n_used, n_slots = _slot_plan(route, counts, 256)
        x_init = jnp.zeros((n_slots, d), F32) if layer == 0 else x_sorted
        x_sorted = _dispatch(h2, dest, x_init)
        y_sorted = _experts(x_sorted, blk_exp, n_used, moe_w1, moe_w3, moe_w2, layer)
        xs = _combine(xs, route, mod, final_g, y_sorted, dest, n_ctx, last)

    return xs
```

```python
import functools
import math

import jax
import jax.numpy as jnp
from jax import lax
from jax.experimental import pallas as pl
from jax.experimental.pallas import tpu as pltpu

F32 = jnp.float32
BF16 = jnp.bfloat16

D_MODEL = 1024
GRID_W = 64
CHUNK = 128
NORM_EPS = 1e-6
ROPE_BASE = 10000.0

SCAN_HEADS = 4
SCAN_DK = 128
SCAN_DV = 256
GLA_LOW_RANK = 16
GLA_TAU = 16.0

ATT_HEAD_DIM = 128
ATT_Q_HEADS = 8
ATT_KV_HEADS = 2
ATT_GROUP = ATT_Q_HEADS // ATT_KV_HEADS

MOE_GROUPS = 4
MOE_EPG = 8
MOE_EXPERTS = 32
MOE_BLOCK = 512
EXPERT_LANE0 = 32

LANES = 128
INPROJ_TN = 2816
NEG = -1e30

_REF_LAYOUT = (
    ('gla_q', 512), ('gla_k', 512), ('gla_v', 1024), ('gla_r', 1024), ('gla_lr', 32),
    ('att_q', 1024), ('att_k', 256), ('att_v', 256),
    ('ret_q', 512), ('ret_k', 512), ('ret_v', 1024), ('ret_g', 1024), ('gates', 3072),
)
_OUR_ORDER = ('gla_v', 'gla_r', 'att_q', 'ret_v', 'ret_g', 'gates', 'gla_q', 'gla_k', 'ret_q', 'ret_k',
              'att_k', 'att_v', 'gla_lr')


def _layout():
    ref_off, start = {}, 0
    for name, width in _REF_LAYOUT:
        ref_off[name] = (start, width)
        start += width
    col, off = {}, 0
    for name in _OUR_ORDER:
        col[name] = off
        off += ref_off[name][1]
    n_p = (off + INPROJ_TN - 1) // INPROJ_TN * INPROJ_TN
    return ref_off, col, n_p


_REF_OFF, COL, N_P = _layout()
VMEM_LIMIT = 56 * 1024 * 1024


def _cparams(sem):
    return pltpu.CompilerParams(dimension_semantics=sem, vmem_limit_bytes=VMEM_LIMIT)


def _silu(x):
    return x / (1.0 + jnp.exp(-x))


def _sigmoid(x):
    return 1.0 / (1.0 + jnp.exp(-x))


def _dot(a, b):
    return jnp.dot(a, b, preferred_element_type=F32)


def _dot_nt(a, b):
    return lax.dot_general(a, b, (((1,), (1,)), ((), ())), preferred_element_type=F32)


def _dot_tn(a, b):
    return lax.dot_general(a, b, (((0,), (0,)), ((), ())), preferred_element_type=F32)


def _mod_kernel(c_ref, w_ref, b_ref, o_ref):
    s = _silu(c_ref[...]).astype(BF16)
    o_ref[...] = _dot(s, w_ref[...].astype(BF16)) + b_ref[...]


def _modulation(cc, w_ada, b_ada):
    depth, d, n6 = w_ada.shape
    rows = cc.shape[0]
    tn = 512
    return pl.pallas_call(
        _mod_kernel,
        grid=(depth, n6 // tn),
        in_specs=[pl.BlockSpec((rows, d), lambda l, j: (0, 0)),
                  pl.BlockSpec((None, d, tn), lambda l, j: (l, 0, j)),
                  pl.BlockSpec((None, 1, tn), lambda l, j: (l, 0, j))],
        out_specs=pl.BlockSpec((None, rows, tn), lambda l, j: (l, 0, j)),
        out_shape=jax.ShapeDtypeStruct((depth, rows, n6), F32),
        compiler_params=_cparams(("arbitrary", "arbitrary")),
        name="modulation",
    )(cc, w_ada, b_ada.reshape(depth, 1, n6))


def _inproj_kernel(x_ref, mod_ref, g_ref, w_ref, o_ref, h_scr, *, n_ctx, tm):
    t = pl.program_id(1)

    @pl.when(pl.program_id(2) == 0)
    def _():
        x = x_ref[...]
        y = x * lax.rsqrt(jnp.mean(x * x, axis=-1, keepdims=True) + NORM_EPS) * g_ref[...]
        row = t * tm + lax.broadcasted_iota(jnp.int32, (tm, 1), 0)
        is_ctx = row < n_ctx
        shift = jnp.where(is_ctx, mod_ref[0, 0:1, :], mod_ref[1, 0:1, :])
        scale = jnp.where(is_ctx, mod_ref[0, 1:2, :], mod_ref[1, 1:2, :])
        h_scr[...] = (y * (1.0 + scale) + shift).astype(BF16)

    o_ref[...] = _dot(h_scr[...], w_ref[...]).astype(o_ref.dtype)


def _in_projection(xs, mod, g, w_p, n_ctx):
    b_, t_, d = xs.shape
    tm = 1408 if t_ % 1408 == 0 else (768 if t_ % 768 == 0 else 256)
    tn = INPROJ_TN
    return pl.pallas_call(
        functools.partial(_inproj_kernel, n_ctx=n_ctx, tm=tm),
        grid=(b_, t_ // tm, N_P // tn),
        in_specs=[pl.BlockSpec((None, tm, d), lambda b, t, j: (b, t, 0)),
                  pl.BlockSpec((None, 2, 6, d), lambda b, t, j: (b, 0, 0, 0)),
                  pl.BlockSpec((1, d), lambda b, t, j: (0, 0)),
                  pl.BlockSpec((d, tn), lambda b, t, j: (0, j))],
        out_specs=pl.BlockSpec((None, tm, tn), lambda b, t, j: (b, t, j)),
        out_shape=jax.ShapeDtypeStruct((b_, t_, N_P), BF16),
        scratch_shapes=[pltpu.VMEM((tm, d), BF16)],
        compiler_params=_cparams(("arbitrary", "arbitrary", "arbitrary")),
        name="in_projection",
    )(xs, mod, g.reshape(1, d), w_p)


def _chunk_update(q_in, k_in, k_out, v_bf, dec, s_ref, idx, mask):
    qb = q_in.astype(BF16)
    sc = jnp.where(mask, _dot_nt(qb, k_in.astype(BF16)), 0.0)
    st = s_ref[idx]
    o = _dot(sc.astype(BF16), v_bf) + _dot_nt(qb, st.astype(BF16))
    s_ref[idx] = st * dec + _dot_tn(v_bf, k_out.astype(BF16))
    return o


def _split3(x):
    hi = x.astype(BF16)
    r = x - hi.astype(F32)
    mid = r.astype(BF16)
    lo = (r - mid.astype(F32)).astype(BF16)
    return hi, mid, lo


def _group_norm_gate(o, g, r):
    mu = jnp.mean(o, axis=-1, keepdims=True)
    oc = o - mu
    var = jnp.mean(oc * oc, axis=-1, keepdims=True)
    return oc * lax.rsqrt(var + NORM_EPS) * g * _silu(r)


SCAN_BATCH_BLOCK = 4


def _scan_kernel(*refs, backward, bb_n):
    if backward:
        (gq, gk, gv, lr, rq, rk, rv, cos, sin, wa2, ba, obg, obr, sg, sr) = refs
    else:
        (gq, gk, gv, lr, rq, rk, rv, cos, sin, wa2, ba, gr, rg, obg, obr, gng, rng, yg, yr, sg, sr) = refs
    c_ = CHUNK

    @pl.when(pl.program_id(1) == 0)
    def _():
        sg[...] = jnp.zeros_like(sg)
        sr[...] = jnp.zeros_like(sr)

    row = lax.broadcasted_iota(jnp.int32, (c_, c_), 0)
    col = lax.broadcasted_iota(jnp.int32, (c_, c_), 1)
    if backward:
        tri = jnp.where(col >= row, 1.0, 0.0).astype(BF16)
        mask = col > row
    else:
        tri = jnp.where(col <= row, 1.0, 0.0).astype(BF16)
        mask = col <= row
    last = 0 if backward else c_ - 1

    pos = lax.broadcasted_iota(jnp.int32, (c_, SCAN_DK), 0)
    steps = ((c_ - pos) if backward else (pos + 1)).astype(F32)
    cosv = cos[...]
    sinv = sin[...]
    scale = SCAN_DK ** -0.5

    for bb in range(bb_n):
        for h in range(SCAN_HEADS):
            ks = slice(h * SCAN_DK, (h + 1) * SCAN_DK)
            vs = slice(h * SCAN_DV, (h + 1) * SCAN_DV)
            ld = math.log(1.0 - 2.0 ** (-5.0 - h))
            c = steps * ld
            q = rq[bb, :, ks].astype(F32)
            k = rk[bb, :, ks].astype(F32)
            q = q * cosv + pltpu.roll(q, SCAN_DK // 2, 1) * sinv
            k = (k * cosv + pltpu.roll(k, SCAN_DK // 2, 1) * sinv) * scale
            dec = jnp.full((1, SCAN_DK), math.exp(c_ * ld), F32)
            o = _chunk_update(q * jnp.exp(c), k * jnp.exp(-c), k * jnp.exp(c_ * ld - c), rv[bb, :, vs], dec,
                              sr, (bb, h), mask)
            if backward:
                obr[bb, :, vs] = o
            else:
                yr[bb, :, vs] = _group_norm_gate(o + obr[bb, :, vs], rng[:, vs],
                                                 rg[bb, :, vs].astype(F32)).astype(yr.dtype)

    for bb in range(bb_n):
        pre = _dot(lr[bb], wa2[...]) + ba[...]
        la = (jnp.minimum(pre, 0.0) - jnp.log1p(jnp.exp(-jnp.abs(pre)))) * (1.0 / GLA_TAU)
        hi, mid, lo = _split3(la)
        cum = _dot(tri, hi) + _dot(tri, mid) + _dot(tri, lo)
        tot = cum[last:last + 1, :]
        for h in range(SCAN_HEADS):
            ks = slice(h * SCAN_DK, (h + 1) * SCAN_DK)
            vs = slice(h * SCAN_DV, (h + 1) * SCAN_DV)
            c = cum[:, ks]
            th = tot[:, ks]
            q = gq[bb, :, ks].astype(F32) * scale
            k = gk[bb, :, ks].astype(F32)
            o = _chunk_update(q * jnp.exp(c), k * jnp.exp(-c), k * jnp.exp(th - c), gv[bb, :, vs], jnp.exp(th),
                              sg, (bb, h), mask)
            if backward:
                obg[bb, :, vs] = o
            else:
                yg[bb, :, vs] = _group_norm_gate(o + obg[bb, :, vs], gng[:, vs],
                                                 gr[bb, :, vs].astype(F32)).astype(yg.dtype)


def _scan(p, ret_cos, ret_sin, wa2_pad, ba, n_ctx, backward, extras=None):
    b_, t_, _ = p.shape
    n_chunks = t_ // CHUNK
    n_cc = n_ctx // CHUNK
    d = D_MODEL
    bb_n = SCAN_BATCH_BLOCK if b_ % SCAN_BATCH_BLOCK == 0 else 1

    if backward:
        def cidx(s):
            return jnp.where(s < n_cc, n_cc - 1 - s, n_chunks - 1 + n_cc - s)
    else:
        def cidx(s):
            return s

    def pspec(name, width):
        blk = COL[name] // width
        return pl.BlockSpec((bb_n, CHUNK, width), lambda b, s: (b, cidx(s), blk))

    dirn = 1 if backward else 0
    tok = lambda width: pl.BlockSpec((bb_n, CHUNK, width), lambda b, s: (b, cidx(s), 0))
    in_specs = [pspec('gla_q', 512), pspec('gla_k', 512), pspec('gla_v', 1024), pspec('gla_lr', LANES),
                pspec('ret_q', 512), pspec('ret_k', 512), pspec('ret_v', 1024),
                pl.BlockSpec((CHUNK, SCAN_DK), lambda b, s: (cidx(s), 0)),
                pl.BlockSpec((CHUNK, SCAN_DK), lambda b, s: (cidx(s), 0)),
                pl.BlockSpec((None, LANES, 512), lambda b, s: (dirn, 0, 0)),
                pl.BlockSpec((None, 1, 512), lambda b, s: (dirn, 0, 0))]
    args = [p, p, p, p, p, p, p, ret_cos, ret_sin, wa2_pad, ba]
    state = [pltpu.VMEM((bb_n, SCAN_HEADS, SCAN_DV, SCAN_DK), F32),
             pltpu.VMEM((bb_n, SCAN_HEADS, SCAN_DV, SCAN_DK), F32)]
    if backward:
        out_specs = [tok(d), tok(d)]
        out_shape = [jax.ShapeDtypeStruct((b_, t_, d), F32)] * 2
    else:
        obg, obr, gng, rng = extras
        in_specs += [pspec('gla_r', 1024), pspec('ret_g', 1024), tok(d), tok(d),
                     pl.BlockSpec((1, d), lambda b, s: (0, 0)), pl.BlockSpec((1, d), lambda b, s: (0, 0))]
        args += [p, p, obg, obr, gng.reshape(1, d), rng.reshape(1, d)]
        out_specs = [tok(d), tok(d)]
        out_shape = [jax.ShapeDtypeStruct((b_, t_, d), BF16)] * 2
    return pl.pallas_call(
        functools.partial(_scan_kernel, backward=backward, bb_n=bb_n),
        grid=(b_ // bb_n, n_chunks),
        in_specs=in_specs, out_specs=out_specs, out_shape=out_shape,
        scratch_shapes=state,
        compiler_params=_cparams(("arbitrary", "arbitrary")),
        name="scan_bwd" if backward else "scan_fwd",
    )(*args)


def _rope_axial(x, cosv, sinv, lane_lo):
    partner = jnp.where(lane_lo, pltpu.roll(x, 96, 1), pltpu.roll(x, 32, 1))
    return x * cosv + partner * sinv


def _attn_block(refs, bb, local, first, last):
    (sink, q, kp, ks, kn, vp, vs, vn, kc, vc, cp, cs, cn, sp, ss, sn, o_ref) = refs
    qb_ = q.shape[1]
    hd = ATT_HEAD_DIM
    scale = hd ** -0.5
    rows = ATT_GROUP * qb_
    rowi = lax.broadcasted_iota(jnp.int32, (rows, 1), 0)
    if local:
        lane = lax.broadcasted_iota(jnp.int32, (qb_, hd), 1)
        lane_lo = (lane & 63) < 32
        i_idx = lax.broadcasted_iota(jnp.int32, (rows, qb_), 0) & (qb_ - 1)
        j_idx = lax.broadcasted_iota(jnp.int32, (rows, qb_), 1)
        mask_p = jnp.logical_and(j_idx >= i_idx, jnp.logical_not(first))
        mask_n = jnp.logical_and(j_idx <= i_idx, jnp.logical_not(last))

    for g in range(ATT_KV_HEADS):
        gs = slice(g * hd, (g + 1) * hd)
        heads = []
        for hh in range(ATT_GROUP):
            qs = slice((g * ATT_GROUP + hh) * hd, (g * ATT_GROUP + hh + 1) * hd)
            qh = q[bb, :, qs].astype(F32)
            if local:
                qh = _rope_axial(qh, cs[...], ss[...], lane_lo)
            heads.append(qh.astype(BF16))
        q4 = jnp.concatenate(heads, axis=0)
        sink_col = jnp.full((rows, 1), sink[g * ATT_GROUP], F32)
        for hh in range(1, ATT_GROUP):
            sink_col = jnp.where(rowi >= hh * qb_, sink[g * ATT_GROUP + hh], sink_col)

        s_c = _dot_nt(q4, kc[bb, :, gs]) * scale
        n_c = s_c.shape[1] // qb_
        m_el = s_c[:, 0:qb_]
        for cpart in range(1, n_c):
            m_el = jnp.maximum(m_el, s_c[:, cpart * qb_:(cpart + 1) * qb_])
        if local:
            k_p = _rope_axial(kp[bb, :, gs].astype(F32), cp[...], sp[...], lane_lo).astype(BF16)
            k_s = _rope_axial(ks[bb, :, gs].astype(F32), cs[...], ss[...], lane_lo).astype(BF16)
            k_n = _rope_axial(kn[bb, :, gs].astype(F32), cn[...], sn[...], lane_lo).astype(BF16)
            s_p = jnp.where(mask_p, _dot_nt(q4, k_p) * scale, NEG)
            s_s = _dot_nt(q4, k_s) * scale
            s_n = jnp.where(mask_n, _dot_nt(q4, k_n) * scale, NEG)
            m_el = jnp.maximum(jnp.maximum(m_el, s_p), jnp.maximum(s_s, s_n))
        m = jnp.maximum(sink_col, jnp.max(m_el, axis=-1, keepdims=True))
        p_c = jnp.exp(s_c - m)
        d_el = p_c[:, 0:qb_]
        for cpart in range(1, n_c):
            d_el = d_el + p_c[:, cpart * qb_:(cpart + 1) * qb_]
        acc = _dot(p_c.astype(BF16), vc[bb, :, gs])
        if local:
            p_p = jnp.exp(s_p - m)
            p_s = jnp.exp(s_s - m)
            p_n = jnp.exp(s_n - m)
            d_el = d_el + p_p + p_s + p_n
            acc = acc + _dot(p_p.astype(BF16), vp[bb, :, gs]) + _dot(p_s.astype(BF16), vs[bb, :, gs]) \
                + _dot(p_n.astype(BF16), vn[bb, :, gs])
        den = jnp.exp(sink_col - m) + jnp.sum(d_el, axis=-1, keepdims=True)
        out = acc / den
        for hh in range(ATT_GROUP):
            qs = slice((g * ATT_GROUP + hh) * hd, (g * ATT_GROUP + hh + 1) * hd)
            o_ref[bb, :, qs] = out[hh * qb_:(hh + 1) * qb_, :].astype(o_ref.dtype)


ATTN_BATCH_BLOCK = 4


def _attn_kernel(*refs, n_cc, nb, bb_n):
    n = pl.program_id(1)

    @pl.when(n < n_cc)
    def _():
        for bb in range(bb_n):
            _attn_block(refs, bb, False, None, None)

    @pl.when(n >= n_cc)
    def _():
        for bb in range(bb_n):
            _attn_block(refs, bb, True, n == n_cc, n == n_cc + nb - 1)


def _attention(p, sink, att_cos, att_sin, n_ctx):
    b_, t_, _ = p.shape
    qb_ = 128
    n_cc = n_ctx // qb_
    nb = (t_ - n_ctx) // qb_
    d = D_MODEL
    bb_n = ATTN_BATCH_BLOCK if b_ % ATTN_BATCH_BLOCK == 0 else 1
    qblk, kblk, vblk = COL['att_q'] // 1024, COL['att_k'] // 256, COL['att_v'] // 256
    smem = pl.BlockSpec(memory_space=pltpu.SMEM)

    prev = lambda n: jnp.maximum(n - 1, n_cc)
    this = lambda n: jnp.maximum(n, n_cc)
    nxt = lambda n: jnp.maximum(jnp.minimum(n + 1, n_cc + nb - 1), n_cc)

    def tspec(width, blk, fn):
        return pl.BlockSpec((bb_n, qb_, width), lambda b, n: (b, fn(n), blk))

    def tab(fn):
        return pl.BlockSpec((qb_, ATT_HEAD_DIM), lambda b, n: (fn(n) - n_cc, 0))

    ctx_k = pl.BlockSpec((bb_n, n_ctx, 256), lambda b, n: (b, 0, kblk))
    ctx_v = pl.BlockSpec((bb_n, n_ctx, 256), lambda b, n: (b, 0, vblk))
    return pl.pallas_call(
        functools.partial(_attn_kernel, n_cc=n_cc, nb=nb, bb_n=bb_n),
        grid=(b_ // bb_n, n_cc + nb),
        in_specs=[smem, tspec(1024, qblk, lambda n: n),
                  tspec(256, kblk, prev), tspec(256, kblk, this), tspec(256, kblk, nxt),
                  tspec(256, vblk, prev), tspec(256, vblk, this), tspec(256, vblk, nxt),
                  ctx_k, ctx_v, tab(prev), tab(this), tab(nxt), tab(prev), tab(this), tab(nxt)],
        out_specs=pl.BlockSpec((bb_n, qb_, d), lambda b, n: (b, n, 0)),
        out_shape=jax.ShapeDtypeStruct((b_, t_, d), BF16),
        compiler_params=_cparams(("arbitrary", "arbitrary")),
        name="attention",
    )(sink, p, p, p, p, p, p, p, p, p, att_cos, att_cos, att_cos, att_sin, att_sin, att_sin)


def _merge_half(rs, mod_ref, cnt, yg, ya, yr, pg, pa, pr, x_ref, g2_ref, wg, wa, wr, wo, wrh, wrl, br,
                xo_ref, h2_ref, route_ref, tm):
    merged = (_sigmoid(pg[rs, :].astype(F32)) * _dot(yg[rs, :], wg[...])
              + _sigmoid(pa[rs, :].astype(F32)) * _dot(ya[rs, :], wa[...])
              + _sigmoid(pr[rs, :].astype(F32)) * _dot(yr[rs, :], wr[...]))
    mix = _dot(merged.astype(BF16), wo[...])
    x = x_ref[rs, :] + mod_ref[2:3, :] * mix
    xo_ref[rs, :] = x
    y = x * lax.rsqrt(jnp.mean(x * x, axis=-1, keepdims=True) + NORM_EPS) * g2_ref[...]
    h2 = y * (1.0 + mod_ref[4:5, :]) + mod_ref[3:4, :]
    h2_ref[rs, :] = h2

    h_hi = h2.astype(BF16)
    h_lo = (h2 - h_hi.astype(F32)).astype(BF16)
    logits = _dot(h_hi, wrh[...]) + _dot(h_lo, wrh[...]) + _dot(h_hi, wrl[...]) + br[...]

    lane = lax.broadcasted_iota(jnp.int32, (tm, LANES), 1)
    lanef = lane.astype(F32)
    is_grp = lane < MOE_GROUPS
    gl = jnp.where(is_grp, logits, NEG)
    gmax = jnp.max(gl, axis=-1, keepdims=True)
    gidx = jnp.min(jnp.where(gl == gmax, lanef, float(LANES)), axis=-1, keepdims=True)
    gprob = 1.0 / jnp.sum(jnp.where(is_grp, jnp.exp(gl - gmax), 0.0), axis=-1, keepdims=True)
    lo = EXPERT_LANE0 + MOE_EPG * gidx
    el = jnp.where(jnp.logical_and(lanef >= lo, lanef < lo + MOE_EPG), logits, NEG)
    t1 = jnp.max(el, axis=-1, keepdims=True)
    i1 = jnp.min(jnp.where(el == t1, lanef, float(LANES)), axis=-1, keepdims=True)
    el2 = jnp.where(lanef == i1, NEG, el)
    t2 = jnp.max(el2, axis=-1, keepdims=True)
    i2 = jnp.min(jnp.where(el2 == t2, lanef, float(LANES)), axis=-1, keepdims=True)
    ex = jnp.exp(t2 - t1)
    w0 = gprob / (1.0 + ex)
    w1 = gprob * ex / (1.0 + ex)

    oh0 = lanef == i1
    oh1 = lanef == i2
    both = jnp.where(jnp.logical_or(oh0, oh1), 1.0, 0.0)
    rr = lax.broadcasted_iota(jnp.int32, (tm, tm), 0)
    cc = lax.broadcasted_iota(jnp.int32, (tm, tm), 1)
    tri = jnp.where(cc < rr, 1.0, 0.0).astype(BF16)
    base = cnt + _dot(tri, both.astype(BF16))
    r0 = jnp.sum(jnp.where(oh0, base, 0.0), axis=-1, keepdims=True)
    r1 = jnp.sum(jnp.where(oh1, base, 0.0), axis=-1, keepdims=True)

    out = jnp.where(lane == 0, i1 - EXPERT_LANE0, 0.0)
    out = jnp.where(lane == 1, i2 - EXPERT_LANE0, out)
    out = jnp.where(lane == 2, w0, out)
    out = jnp.where(lane == 3, w1, out)
    out = jnp.where(lane == 4, r0, out)
    out = jnp.where(lane == 5, r1, out)
    route_ref[rs, :] = out
    return cnt + jnp.sum(both, axis=0, keepdims=True)


def _merge_kernel(*refs, tm, n_half):
    (yg, ya, yr, pg, pa, pr, x_ref), mods = refs[:7], refs[7:7 + n_half]
    (g2_ref, wg, wa, wr, wo, wrh, wrl, br, xo_ref, h2_ref, route_ref, cnt_ref, cnt_scr) = refs[7 + n_half:]

    @pl.when(pl.program_id(0) == 0)
    def _():
        cnt_scr[...] = jnp.zeros_like(cnt_scr)

    cnt = cnt_scr[...]
    for a in range(n_half):
        cnt = _merge_half(slice(a * tm, (a + 1) * tm), mods[a], cnt, yg, ya, yr, pg, pa, pr, x_ref, g2_ref,
                          wg, wa, wr, wo, wrh, wrl, br, xo_ref, h2_ref, route_ref, tm)
    cnt_scr[...] = cnt
    cnt_ref[...] = jnp.broadcast_to(cnt, cnt_ref.shape)


def _merge(yg, ya, yr, p, xs, mod, g2, wg, wa, wr, wo, wrh, wrl, br, n_ctx):
    b_, t_, d = xs.shape
    n_tok = b_ * t_
    tm = 256
    nt = t_ // tm
    n_ct = n_ctx // tm
    n_half = 2 if n_tok % (2 * tm) == 0 else 1
    tile = n_half * tm
    gblk = COL['gates'] // d
    tok = pl.BlockSpec((tile, d), lambda g: (g, 0))
    wspec = pl.BlockSpec((d, d), lambda g: (0, 0))
    rspec = pl.BlockSpec((d, LANES), lambda g: (0, 0))

    def gate(i):
        return pl.BlockSpec((tile, d), lambda g: (g, gblk + i))

    def mspec(a):
        def imap(g):
            run = g * n_half + a
            return run // nt, jnp.where(run % nt < n_ct, 0, 1), 0, 0
        return pl.BlockSpec((None, None, 6, d), imap)

    flat = lambda arr: arr.reshape(n_tok, arr.shape[-1])
    xo, h2, route, counts = pl.pallas_call(
        functools.partial(_merge_kernel, tm=tm, n_half=n_half),
        grid=(n_tok // tile,),
        in_specs=[tok, tok, tok, gate(0), gate(1), gate(2), tok] + [mspec(a) for a in range(n_half)]
        + [pl.BlockSpec((1, d), lambda g: (0, 0)), wspec, wspec, wspec, wspec, rspec, rspec,
           pl.BlockSpec((1, LANES), lambda g: (0, 0))],
        out_specs=[tok, tok, pl.BlockSpec((tile, LANES), lambda g: (g, 0)),
                   pl.BlockSpec((8, LANES), lambda g: (0, 0))],
        out_shape=[jax.ShapeDtypeStruct((n_tok, d), F32), jax.ShapeDtypeStruct((n_tok, d), F32),
                   jax.ShapeDtypeStruct((n_tok, LANES), F32), jax.ShapeDtypeStruct((8, LANES), F32)],
        scratch_shapes=[pltpu.VMEM((1, LANES), F32)],
        compiler_params=_cparams(("arbitrary",)),
        name="merge_route",
    )(flat(yg), flat(ya), flat(yr), flat(p), flat(p), flat(p), flat(xs), *([mod] * n_half),
      g2.reshape(1, d), wg, wa, wr, wo, wrh, wrl, br)
    return xo.reshape(b_, t_, d), h2.reshape(b_, t_, d), route.reshape(b_, t_, LANES), counts


ROW_DMA_UNROLL = 16


def _issue_rows(row_copy, tm):
    def issue(c, carry):
        for u in range(ROW_DMA_UNROLL):
            row_copy(c * ROW_DMA_UNROLL + u, 0).start()
            row_copy(c * ROW_DMA_UNROLL + u, 1).start()
        return carry

    lax.fori_loop(0, tm // ROW_DMA_UNROLL, issue, 0)


def _drain_rows(row_copy, tm):
    def drain(c, carry):
        for u in range(ROW_DMA_UNROLL):
            row_copy(c * ROW_DMA_UNROLL + u, 0).wait()
            row_copy(c * ROW_DMA_UNROLL + u, 1).wait()
        return carry

    lax.fori_loop(0, tm // ROW_DMA_UNROLL, drain, 0)


def _dispatch_kernel(dest_ref, h_ref, init_ref, xs_ref, sem, *, tm):
    del init_ref

    def row_copy(i, k):
        return pltpu.make_async_copy(h_ref.at[pl.ds(i, 1)], xs_ref.at[pl.ds(dest_ref[0, k * tm + i], 1)], sem)

    _issue_rows(row_copy, tm)
    _drain_rows(row_copy, tm)


def _dispatch(h2, dest, x_init):
    b_, t_, d = h2.shape
    n_slots = x_init.shape[0]
    tm = 256
    nt = t_ // tm
    return pl.pallas_call(
        functools.partial(_dispatch_kernel, tm=tm),
        grid=(b_, nt),
        in_specs=[pl.BlockSpec((None, 1, 2 * tm), lambda b, t: (b * nt + t, 0, 0), memory_space=pltpu.SMEM),
                  pl.BlockSpec((tm, d), lambda b, t: (b * nt + t, 0)),
                  pl.BlockSpec(memory_space=pl.ANY)],
        out_specs=pl.BlockSpec(memory_space=pl.ANY),
        out_shape=jax.ShapeDtypeStruct((n_slots, d), F32),
        input_output_aliases={2: 0},
        scratch_shapes=[pltpu.SemaphoreType.DMA],
        compiler_params=_cparams(("arbitrary", "arbitrary")),
        name="moe_dispatch",
    )(dest, h2.reshape(b_ * t_, d), x_init)


def _expert_kernel(be_ref, nu_ref, x_ref, w1_ref, w3_ref, w2_ref, y_ref, wb):
    i = pl.program_id(0)
    used = i < nu_ref[0]
    new_expert = jnp.logical_or(i == 0, be_ref[i] != be_ref[jnp.maximum(i - 1, 0)])

    @pl.when(jnp.logical_and(used, new_expert))
    def _():
        wb[0] = w1_ref[...].astype(BF16)
        wb[1] = w3_ref[...].astype(BF16)
        wb[2] = w2_ref[...].astype(BF16)

    @pl.when(used)
    def _():
        xb = x_ref[...].astype(BF16)
        h1 = _dot(xb, wb[0])
        h3 = _dot(xb, wb[1])
        y_ref[...] = _dot((_silu(h1) * h3).astype(BF16), wb[2])

    @pl.when(i >= nu_ref[0])
    def _():
        y_ref[...] = jnp.zeros_like(y_ref)


def _experts(x_sorted, blk_exp, n_used, w1, w3, w2, layer):
    n_slots, d = x_sorted.shape
    hid = w1.shape[-1]
    n_blocks = n_slots // MOE_BLOCK
    grid_spec = pltpu.PrefetchScalarGridSpec(
        num_scalar_prefetch=2,
        grid=(n_blocks,),
        in_specs=[pl.BlockSpec((MOE_BLOCK, d), lambda i, be, nu: (i, 0)),
                  pl.BlockSpec((None, None, d, hid), lambda i, be, nu: (layer, be[i], 0, 0)),
                  pl.BlockSpec((None, None, d, hid), lambda i, be, nu: (layer, be[i], 0, 0)),
                  pl.BlockSpec((None, None, hid, d), lambda i, be, nu: (layer, be[i], 0, 0))],
        out_specs=pl.BlockSpec((MOE_BLOCK, d), lambda i, be, nu: (i, 0)),
        scratch_shapes=[pltpu.VMEM((3, d, hid), BF16)],
    )
    assert d == hid
    return pl.pallas_call(
        _expert_kernel,
        grid_spec=grid_spec,
        out_shape=jax.ShapeDtypeStruct((n_slots, d), F32),
        compiler_params=_cparams(("arbitrary",)),
        name="moe_experts",
    )(blk_exp, n_used, x_sorted, w1, w3, w2)


def _combine_kernel(dcur_ref, dnext_ref, x_ref, route_ref, mod_ref, fg_ref, y_ref, o_ref, buf, sem, *,
                    tm, final, n_steps):
    g = pl.program_id(0)
    slot = g % 2

    def row_copy(dest_ref, sl):
        def make(i, k):
            return pltpu.make_async_copy(y_ref.at[pl.ds(dest_ref[0, k * tm + i], 1)],
                                         buf.at[sl, k, pl.ds(i, 1)], sem.at[sl])
        return make

    @pl.when(g == 0)
    def _():
        _issue_rows(row_copy(dcur_ref, 0), tm)

    @pl.when(g + 1 < n_steps)
    def _():
        _issue_rows(row_copy(dnext_ref, 1 - slot), tm)

    _drain_rows(row_copy(dcur_ref, slot), tm)

    r = route_ref[...]
    f = buf[slot, 0] * r[:, 2:3] + buf[slot, 1] * r[:, 3:4]
    x = x_ref[...] + mod_ref[5:6, :] * f
    if final:
        x = x * lax.rsqrt(jnp.mean(x * x, axis=-1, keepdims=True) + NORM_EPS) * fg_ref[...]
    o_ref[...] = x


def _combine(xs, route, mod, final_g, y_sorted, dest, n_ctx, final):
    b_, t_, d = xs.shape
    tm = 256
    nt = t_ // tm
    n_ct = n_ctx // tm
    skip = n_ct if final else 0
    ntl = nt - skip
    n_steps = b_ * ntl

    def bt(g):
        return g // ntl, g % ntl + skip

    def tok(w):
        return pl.BlockSpec((None, tm, w), lambda g: (*bt(g), 0))

    def dspec(off):
        def imap(g):
            b, t = bt(jnp.minimum(g + off, n_steps - 1))
            return b * nt + t, 0, 0
        return pl.BlockSpec((None, 1, 2 * tm), imap, memory_space=pltpu.SMEM)

    return pl.pallas_call(
        functools.partial(_combine_kernel, tm=tm, final=final, n_steps=n_steps),
        grid=(n_steps,),
        in_specs=[dspec(0), dspec(1), tok(d), tok(LANES),
                  pl.BlockSpec((None, None, 6, d), lambda g: (g // ntl, jnp.where(g % ntl + skip < n_ct, 0, 1), 0, 0)),
                  pl.BlockSpec((1, d), lambda g: (0, 0)),
                  pl.BlockSpec(memory_space=pl.ANY)],
        out_specs=pl.BlockSpec((None, tm, d), lambda g: (g // ntl, g % ntl, 0)),
        out_shape=jax.ShapeDtypeStruct((b_, t_ - skip * tm, d), F32),
        scratch_shapes=[pltpu.VMEM((2, 2, tm, d), F32), pltpu.SemaphoreType.DMA((2,))],
        compiler_params=_cparams(("arbitrary",)),
        name="moe_combine",
    )(dest, dest, xs, route, mod, final_g.reshape(1, d), y_sorted)


def _slot_plan(route, counts, tm):
    b_, t_, _ = route.shape
    n_tok = b_ * t_
    cnt = counts[0, EXPERT_LANE0:EXPERT_LANE0 + MOE_EXPERTS].astype(jnp.int32)
    padded = (cnt + MOE_BLOCK - 1) // MOE_BLOCK * MOE_BLOCK
    pad_end = jnp.cumsum(padded)
    pad_start = pad_end - padded
    n_blocks = (n_tok * 2 + MOE_EXPERTS * (MOE_BLOCK - 1) + MOE_BLOCK - 1) // MOE_BLOCK
    n_used = (pad_end[-1] // MOE_BLOCK).astype(jnp.int32)
    blk = jnp.minimum(jnp.arange(n_blocks, dtype=jnp.int32), n_used - 1) * MOE_BLOCK
    blk_exp = jnp.sum((pad_end[None, :] <= blk[:, None]).astype(jnp.int32), axis=1)
    blk_exp = jnp.minimum(blk_exp, MOE_EXPERTS - 1).astype(jnp.int32)
    e = route[..., 0:2].astype(jnp.int32)
    rank = route[..., 4:6].astype(jnp.int32)
    eid = jnp.arange(MOE_EXPERTS, dtype=jnp.int32)
    dest = rank + jnp.sum(jnp.where(e[..., None] == eid, pad_start, 0), axis=-1)
    dest = dest.reshape(b_ * t_ // tm, tm, 2).transpose(0, 2, 1).reshape(b_ * t_ // tm, 1, 2 * tm)
    return dest, blk_exp, n_used.reshape(1), n_blocks * MOE_BLOCK


def _permute_w_in(w_in):
    parts = [w_in[:, _REF_OFF[name][0]:_REF_OFF[name][0] + _REF_OFF[name][1]] for name in _OUR_ORDER]
    width = sum(part.shape[1] for part in parts)
    parts.append(jnp.zeros((w_in.shape[0], N_P - width), w_in.dtype))
    return jnp.concatenate(parts, axis=1).astype(BF16)


def _rope_tables(n_ctx, l_):
    half = SCAN_DK // 2
    inv = ROPE_BASE ** (-jnp.arange(half, dtype=F32) / half)
    ang = jnp.arange(n_ctx + l_, dtype=F32)[:, None] * inv[None, :]
    ret_cos = jnp.concatenate([jnp.cos(ang), jnp.cos(ang)], axis=-1)
    ret_sin = jnp.concatenate([-jnp.sin(ang), jnp.sin(ang)], axis=-1)
    q = ATT_HEAD_DIM // 4
    inv = ROPE_BASE ** (-jnp.arange(q, dtype=F32) / q)
    tpos = jnp.arange(l_)
    a_r = (tpos // GRID_W).astype(F32)[:, None] * inv[None, :]
    a_c = (tpos % GRID_W).astype(F32)[:, None] * inv[None, :]
    att_cos = jnp.concatenate([jnp.cos(a_r), jnp.cos(a_r), jnp.cos(a_c), jnp.cos(a_c)], axis=-1)
    att_sin = jnp.concatenate([-jnp.sin(a_r), jnp.sin(a_r), -jnp.sin(a_c), jnp.sin(a_c)], axis=-1)
    return ret_cos, ret_sin, att_cos, att_sin


def kernel(x, c, ctx, c_ctx, w_ada, b_ada, norm1_g, norm2_g, w_in, gla_wa2, gla_ba, gla_norm_g, attn_sink,
           ret_norm_g, w_br_gla, w_br_attn, w_br_ret, w_out, moe_w_grp, moe_b_grp, moe_w_exp, moe_b_exp,
           moe_w1, moe_w3, moe_w2, final_g):
    b_, l_, d = x.shape
    n_ctx = ctx.shape[1]
    depth = w_ada.shape[0]
    assert d == D_MODEL and n_ctx % 256 == 0 and l_ % 256 == 0 and b_ <= 8

    xs = jnp.concatenate([ctx, x], axis=1)
    cc = jnp.zeros((16, d), F32).at[:b_].set(c).at[b_].set(c_ctx)
    mod_all = _modulation(cc, w_ada, b_ada)
    ret_cos, ret_sin, att_cos, att_sin = _rope_tables(n_ctx, l_)

    for layer in range(depth):
        last = layer == depth - 1
        m = mod_all[layer].reshape(16, 6, d)
        mod = jnp.stack([jnp.broadcast_to(m[b_], (b_, 6, d)), m[:b_]], axis=1)

        w_p = _permute_w_in(w_in[layer])
        wa2_pad = jnp.zeros((2, LANES, 512), F32)
        wa2_pad = wa2_pad.at[0, 0:GLA_LOW_RANK].set(gla_wa2[layer, 0])
        wa2_pad = wa2_pad.at[1, GLA_LOW_RANK:2 * GLA_LOW_RANK].set(gla_wa2[layer, 1]).astype(BF16)
        ba = gla_ba[layer].reshape(2, 1, 512)

        p = _in_projection(xs, mod, norm1_g[layer], w_p, n_ctx)
        obg, obr = _scan(p, ret_cos, ret_sin, wa2_pad, ba, n_ctx, True)
        yg, yr = _scan(p, ret_cos, ret_sin, wa2_pad, ba, n_ctx, False,
                       (obg, obr, gla_norm_g[layer], ret_norm_g[layer]))
        ya = _attention(p, attn_sink[layer], att_cos, att_sin, n_ctx)

        w_route = jnp.zeros((d, LANES), F32)
        w_route = w_route.at[:, 0:MOE_GROUPS].set(moe_w_grp[layer])
        w_route = w_route.at[:, EXPERT_LANE0:EXPERT_LANE0 + MOE_EXPERTS].set(moe_w_exp[layer])
        wr_hi = w_route.astype(BF16)
        wr_lo = (w_route - wr_hi.astype(F32)).astype(BF16)
        b_route = jnp.zeros((1, LANES), F32)
        b_route = b_route.at[0, 0:MOE_GROUPS].set(moe_b_grp[layer])
        b_route = b_route.at[0, EXPERT_LANE0:EXPERT_LANE0 + MOE_EXPERTS].set(moe_b_exp[layer])

        xs, h2, route, counts = _merge(
            yg, ya, yr, p, xs, mod, norm2_g[layer],
            w_br_gla[layer].astype(BF16), w_br_attn[layer].astype(BF16), w_br_ret[layer].astype(BF16),
            w_out[layer].astype(BF16), wr_hi, wr_lo, b_route, n_ctx)

        dest, blk_exp, n_used, n_slots = _slot_plan(route, counts, 256)
        x_init = jnp.zeros((n_slots, d), F32) if layer == 0 else x_sorted
        x_sorted = _dispatch(h2, dest, x_init)
        y_sorted = _experts(x_sorted, blk_exp, n_used, moe_w1, moe_w3, moe_w2, layer)
        xs = _combine(xs, route, mod, final_g, y_sorted, dest, n_ctx, last)

    return xs
```

```python
import functools
import math

import jax
import jax.numpy as jnp
from jax import lax
from jax.experimental import pallas as pl
from jax.experimental.pallas import tpu as pltpu

F32 = jnp.float32
BF16 = jnp.bfloat16

D_MODEL = 1024
GRID_W = 64
CHUNK = 128
NORM_EPS = 1e-6
ROPE_BASE = 10000.0

SCAN_HEADS = 4
SCAN_DK = 128
SCAN_DV = 256
GLA_LOW_RANK = 16
GLA_TAU = 16.0

ATT_HEAD_DIM = 128
ATT_Q_HEADS = 8
ATT_KV_HEADS = 2
ATT_GROUP = ATT_Q_HEADS // ATT_KV_HEADS

MOE_GROUPS = 4
MOE_EPG = 8
MOE_EXPERTS = 32
MOE_BLOCK = 512
EXPERT_LANE0 = 32

LANES = 128
INPROJ_TN = 2816
NEG = -1e30

_REF_LAYOUT = (
    ('gla_q', 512), ('gla_k', 512), ('gla_v', 1024), ('gla_r', 1024), ('gla_lr', 32),
    ('att_q', 1024), ('att_k', 256), ('att_v', 256),
    ('ret_q', 512), ('ret_k', 512), ('ret_v', 1024), ('ret_g', 1024), ('gates', 3072),
)
_OUR_ORDER = ('gla_v', 'gla_r', 'att_q', 'ret_v', 'ret_g', 'gates', 'gla_q', 'gla_k', 'ret_q', 'ret_k',
              'att_k', 'att_v', 'gla_lr')


def _layout():
    ref_off, start = {}, 0
    for name, width in _REF_LAYOUT:
        ref_off[name] = (start, width)
        start += width
    col, off = {}, 0
    for name in _OUR_ORDER:
        col[name] = off
        off += ref_off[name][1]
    n_p = (off + INPROJ_TN - 1) // INPROJ_TN * INPROJ_TN
    return ref_off, col, n_p


_REF_OFF, COL, N_P = _layout()
VMEM_LIMIT = 56 * 1024 * 1024


def _cparams(sem):
    return pltpu.CompilerParams(dimension_semantics=sem, vmem_limit_bytes=VMEM_LIMIT)


def _silu(x):
    return x / (1.0 + jnp.exp(-x))


def _sigmoid(x):
    return 1.0 / (1.0 + jnp.exp(-x))


def _dot(a, b):
    return jnp.dot(a, b, preferred_element_type=F32)


def _dot_nt(a, b):
    return lax.dot_general(a, b, (((1,), (1,)), ((), ())), preferred_element_type=F32)


def _dot_tn(a, b):
    return lax.dot_general(a, b, (((0,), (0,)), ((), ())), preferred_element_type=F32)


def _mod_kernel(c_ref, w_ref, b_ref, o_ref):
    s = _silu(c_ref[...]).astype(BF16)
    o_ref[...] = _dot(s, w_ref[...].astype(BF16)) + b_ref[...]


def _modulation(cc, w_ada, b_ada):
    depth, d, n6 = w_ada.shape
    rows = cc.shape[0]
    tn = 512
    return pl.pallas_call(
        _mod_kernel,
        grid=(depth, n6 // tn),
        in_specs=[pl.BlockSpec((rows, d), lambda l, j: (0, 0)),
                  pl.BlockSpec((None, d, tn), lambda l, j: (l, 0, j)),
                  pl.BlockSpec((None, 1, tn), lambda l, j: (l, 0, j))],
        out_specs=pl.BlockSpec((None, rows, tn), lambda l, j: (l, 0, j)),
        out_shape=jax.ShapeDtypeStruct((depth, rows, n6), F32),
        compiler_params=_cparams(("arbitrary", "arbitrary")),
        name="modulation",
    )(cc, w_ada, b_ada.reshape(depth, 1, n6))


def _inproj_kernel(x_ref, mod_ref, g_ref, w_ref, o_ref, h_scr, *, n_ctx, tm):
    t = pl.program_id(1)

    @pl.when(pl.program_id(2) == 0)
    def _():
        x = x_ref[...]
        y = x * lax.rsqrt(jnp.mean(x * x, axis=-1, keepdims=True) + NORM_EPS) * g_ref[...]
        row = t * tm + lax.broadcasted_iota(jnp.int32, (tm, 1), 0)
        is_ctx = row < n_ctx
        shift = jnp.where(is_ctx, mod_ref[0, 0:1, :], mod_ref[1, 0:1, :])
        scale = jnp.where(is_ctx, mod_ref[0, 1:2, :], mod_ref[1, 1:2, :])
        h_scr[...] = (y * (1.0 + scale) + shift).astype(BF16)

    o_ref[...] = _dot(h_scr[...], w_ref[...]).astype(o_ref.dtype)


def _in_projection(xs, mod, g, w_p, n_ctx):
    b_, t_, d = xs.shape
    tm = 1408 if t_ % 1408 == 0 else (768 if t_ % 768 == 0 else 256)
    tn = INPROJ_TN
    return pl.pallas_call(
        functools.partial(_inproj_kernel, n_ctx=n_ctx, tm=tm),
        grid=(b_, t_ // tm, N_P // tn),
        in_specs=[pl.BlockSpec((None, tm, d), lambda b, t, j: (b, t, 0)),
                  pl.BlockSpec((None, 2, 6, d), lambda b, t, j: (b, 0, 0, 0)),
                  pl.BlockSpec((1, d), lambda b, t, j: (0, 0)),
                  pl.BlockSpec((d, tn), lambda b, t, j: (0, j))],
        out_specs=pl.BlockSpec((None, tm, tn), lambda b, t, j: (b, t, j)),
        out_shape=jax.ShapeDtypeStruct((b_, t_, N_P), BF16),
        scratch_shapes=[pltpu.VMEM((tm, d), BF16)],
        compiler_params=_cparams(("arbitrary", "arbitrary", "arbitrary")),
        name="in_projection",
    )(xs, mod, g.reshape(1, d), w_p)


def _chunk_update(q_in, k_in, k_out, v_bf, dec, s_ref, idx, mask):
    qb = q_in.astype(BF16)
    sc = jnp.where(mask, _dot_nt(qb, k_in.astype(BF16)), 0.0)
    st = s_ref[idx]
    o = _dot(sc.astype(BF16), v_bf) + _dot_nt(qb, st.astype(BF16))
    s_ref[idx] = st * dec + _dot_tn(v_bf, k_out.astype(BF16))
    return o


def _split3(x):
    hi = x.astype(BF16)
    r = x - hi.astype(F32)
    mid = r.astype(BF16)
    lo = (r - mid.astype(F32)).astype(BF16)
    return hi, mid, lo


def _group_norm_gate(o, g, r):
    mu = jnp.mean(o, axis=-1, keepdims=True)
    oc = o - mu
    var = jnp.mean(oc * oc, axis=-1, keepdims=True)
    return oc * lax.rsqrt(var + NORM_EPS) * g * _silu(r)


SCAN_BATCH_BLOCK = 4


def _scan_kernel(*refs, backward, bb_n):
    if backward:
        (gq, gk, gv, lr, rq, rk, rv, cos, sin, wa2, ba, obg, obr, sg, sr) = refs
    else:
        (gq, gk, gv, lr, rq, rk, rv, cos, sin, wa2, ba, gr, rg, obg, obr, gng, rng, yg, yr, sg, sr) = refs
    c_ = CHUNK

    @pl.when(pl.program_id(1) == 0)
    def _():
        sg[...] = jnp.zeros_like(sg)
        sr[...] = jnp.zeros_like(sr)

    row = lax.broadcasted_iota(jnp.int32, (c_, c_), 0)
    col = lax.broadcasted_iota(jnp.int32, (c_, c_), 1)
    if backward:
        tri = jnp.where(col >= row, 1.0, 0.0).astype(BF16)
        mask = col > row
    else:
        tri = jnp.where(col <= row, 1.0, 0.0).astype(BF16)
        mask = col <= row
    last = 0 if backward else c_ - 1

    pos = lax.broadcasted_iota(jnp.int32, (c_, SCAN_DK), 0)
    steps = ((c_ - pos) if backward else (pos + 1)).astype(F32)
    cosv = cos[...]
    sinv = sin[...]
    scale = SCAN_DK ** -0.5

    for bb in range(bb_n):
        for h in range(SCAN_HEADS):
            ks = slice(h * SCAN_DK, (h + 1) * SCAN_DK)
            vs = slice(h * SCAN_DV, (h + 1) * SCAN_DV)
            ld = math.log(1.0 - 2.0 ** (-5.0 - h))
            c = steps * ld
            q = rq[bb, :, ks].astype(F32)
            k = rk[bb, :, ks].astype(F32)
            q = q * cosv + pltpu.roll(q, SCAN_DK // 2, 1) * sinv
            k = (k * cosv + pltpu.roll(k, SCAN_DK // 2, 1) * sinv) * scale
            dec = jnp.full((1, SCAN_DK), math.exp(c_ * ld), F32)
            o = _chunk_update(q * jnp.exp(c), k * jnp.exp(-c), k * jnp.exp(c_ * ld - c), rv[bb, :, vs], dec,
                              sr, (bb, h), mask)
            if backward:
                obr[bb, :, vs] = o
            else:
                yr[bb, :, vs] = _group_norm_gate(o + obr[bb, :, vs], rng[:, vs],
                                                 rg[bb, :, vs].astype(F32)).astype(yr.dtype)

    for bb in range(bb_n):
        pre = _dot(lr[bb], wa2[...]) + ba[...]
        la = (jnp.minimum(pre, 0.0) - jnp.log1p(jnp.exp(-jnp.abs(pre)))) * (1.0 / GLA_TAU)
        hi, mid, lo = _split3(la)
        cum = _dot(tri, hi) + _dot(tri, mid) + _dot(tri, lo)
        tot = cum[last:last + 1, :]
        for h in range(SCAN_HEADS):
            ks = slice(h * SCAN_DK, (h + 1) * SCAN_DK)
            vs = slice(h * SCAN_DV, (h + 1) * SCAN_DV)
            c = cum[:, ks]
            th = tot[:, ks]
            q = gq[bb, :, ks].astype(F32) * scale
            k = gk[bb, :, ks].astype(F32)
            o = _chunk_update(q * jnp.exp(c), k * jnp.exp(-c), k * jnp.exp(th - c), gv[bb, :, vs], jnp.exp(th),
                              sg, (bb, h), mask)
            if backward:
                obg[bb, :, vs] = o
            else:
                yg[bb, :, vs] = _group_norm_gate(o + obg[bb, :, vs], gng[:, vs],
                                                 gr[bb, :, vs].astype(F32)).astype(yg.dtype)


def _scan(p, ret_cos, ret_sin, wa2_pad, ba, n_ctx, backward, extras=None):
    b_, t_, _ = p.shape
    n_chunks = t_ // CHUNK
    n_cc = n_ctx // CHUNK
    d = D_MODEL
    bb_n = SCAN_BATCH_BLOCK if b_ % SCAN_BATCH_BLOCK == 0 else 1

    if backward:
        def cidx(s):
            return jnp.where(s < n_cc, n_cc - 1 - s, n_chunks - 1 + n_cc - s)
    else:
        def cidx(s):
            return s

    def pspec(name, width):
        blk = COL[name] // width
        return pl.BlockSpec((bb_n, CHUNK, width), lambda b, s: (b, cidx(s), blk))

    dirn = 1 if backward else 0
    tok = lambda width: pl.BlockSpec((bb_n, CHUNK, width), lambda b, s: (b, cidx(s), 0))
    in_specs = [pspec('gla_q', 512), pspec('gla_k', 512), pspec('gla_v', 1024), pspec('gla_lr', LANES),
                pspec('ret_q', 512), pspec('ret_k', 512), pspec('ret_v', 1024),
                pl.BlockSpec((CHUNK, SCAN_DK), lambda b, s: (cidx(s), 0)),
                pl.BlockSpec((CHUNK, SCAN_DK), lambda b, s: (cidx(s), 0)),
                pl.BlockSpec((None, LANES, 512), lambda b, s: (dirn, 0, 0)),
                pl.BlockSpec((None, 1, 512), lambda b, s: (dirn, 0, 0))]
    args = [p, p, p, p, p, p, p, ret_cos, ret_sin, wa2_pad, ba]
    state = [pltpu.VMEM((bb_n, SCAN_HEADS, SCAN_DV, SCAN_DK), F32),
             pltpu.VMEM((bb_n, SCAN_HEADS, SCAN_DV, SCAN_DK), F32)]
    if backward:
        out_specs = [tok(d), tok(d)]
        out_shape = [jax.ShapeDtypeStruct((b_, t_, d), F32)] * 2
    else:
        obg, obr, gng, rng = extras
        in_specs += [pspec('gla_r', 1024), pspec('ret_g', 1024), tok(d), tok(d),
                     pl.BlockSpec((1, d), lambda b, s: (0, 0)), pl.BlockSpec((1, d), lambda b, s: (0, 0))]
        args += [p, p, obg, obr, gng.reshape(1, d), rng.reshape(1, d)]
        out_specs = [tok(d), tok(d)]
        out_shape = [jax.ShapeDtypeStruct((b_, t_, d), BF16)] * 2
    return pl.pallas_call(
        functools.partial(_scan_kernel, backward=backward, bb_n=bb_n),
        grid=(b_ // bb_n, n_chunks),
        in_specs=in_specs, out_specs=out_specs, out_shape=out_shape,
        scratch_shapes=state,
        compiler_params=_cparams(("arbitrary", "arbitrary")),
        name="scan_bwd" if backward else "scan_fwd",
    )(*args)


def _rope_axial(x, cosv, sinv, lane_lo):
    partner = jnp.where(lane_lo, pltpu.roll(x, 96, 1), pltpu.roll(x, 32, 1))
    return x * cosv + partner * sinv


def _attn_block(refs, bb, local, first, last):
    (sink, q, kp, ks, kn, vp, vs, vn, kc, vc, cp, cs, cn, sp, ss, sn, cq, sq, o_ref) = refs
    qb_ = q.shape[1]
    hd = ATT_HEAD_DIM
    scale = hd ** -0.5
    rows = ATT_GROUP * qb_
    rowi = lax.broadcasted_iota(jnp.int32, (rows, 1), 0)
    if local:
        lane = lax.broadcasted_iota(jnp.int32, (qb_, hd), 1)
        lane_lo = (lane & 63) < 32
        i_idx = lax.broadcasted_iota(jnp.int32, (rows, qb_), 0) & (qb_ - 1)
        j_idx = lax.broadcasted_iota(jnp.int32, (rows, qb_), 1)
        mask_p = jnp.logical_and(j_idx >= i_idx, jnp.logical_not(first))
        mask_n = jnp.logical_and(j_idx <= i_idx, jnp.logical_not(last))

    for g in range(ATT_KV_HEADS):
        gs = slice(g * hd, (g + 1) * hd)
        heads = []
        for hh in range(ATT_GROUP):
            qs = slice((g * ATT_GROUP + hh) * hd, (g * ATT_GROUP + hh + 1) * hd)
            qh = q[bb, :, qs].astype(F32)
            qh = _rope_axial(qh, cq[...], sq[...], lane_lo) if local else qh * scale
            heads.append(qh.astype(BF16))
        q4 = jnp.concatenate(heads, axis=0)
        sink_col = jnp.full((rows, 1), sink[g * ATT_GROUP], F32)
        for hh in range(1, ATT_GROUP):
            sink_col = jnp.where(rowi >= hh * qb_, sink[g * ATT_GROUP + hh], sink_col)

        s_c = _dot_nt(q4, kc[bb, :, gs])
        n_c = s_c.shape[1] // qb_
        m_el = s_c[:, 0:qb_]
        for cpart in range(1, n_c):
            m_el = jnp.maximum(m_el, s_c[:, cpart * qb_:(cpart + 1) * qb_])
        if local:
            k_p = _rope_axial(kp[bb, :, gs].astype(F32), cp[...], sp[...], lane_lo).astype(BF16)
            k_s = _rope_axial(ks[bb, :, gs].astype(F32), cs[...], ss[...], lane_lo).astype(BF16)
            k_n = _rope_axial(kn[bb, :, gs].astype(F32), cn[...], sn[...], lane_lo).astype(BF16)
            s_p = jnp.where(mask_p, _dot_nt(q4, k_p), NEG)
            s_s = _dot_nt(q4, k_s)
            s_n = jnp.where(mask_n, _dot_nt(q4, k_n), NEG)
            m_el = jnp.maximum(jnp.maximum(m_el, s_p), jnp.maximum(s_s, s_n))
        m = jnp.maximum(sink_col, jnp.max(m_el, axis=-1, keepdims=True))
        p_c = jnp.exp(s_c - m)
        d_el = p_c[:, 0:qb_]
        for cpart in range(1, n_c):
            d_el = d_el + p_c[:, cpart * qb_:(cpart + 1) * qb_]
        acc = _dot(p_c.astype(BF16), vc[bb, :, gs])
        if local:
            p_p = jnp.exp(s_p - m)
            p_s = jnp.exp(s_s - m)
            p_n = jnp.exp(s_n - m)
            d_el = d_el + p_p + p_s + p_n
            acc = acc + _dot(p_p.astype(BF16), vp[bb, :, gs]) + _dot(p_s.astype(BF16), vs[bb, :, gs]) \
                + _dot(p_n.astype(BF16), vn[bb, :, gs])
        den = jnp.exp(sink_col - m) + jnp.sum(d_el, axis=-1, keepdims=True)
        out = acc / den
        for hh in range(ATT_GROUP):
            qs = slice((g * ATT_GROUP + hh) * hd, (g * ATT_GROUP + hh + 1) * hd)
            o_ref[bb, :, qs] = out[hh * qb_:(hh + 1) * qb_, :].astype(o_ref.dtype)


ATTN_BATCH_BLOCK = 4


def _attn_kernel(*refs, n_cc, nb, bb_n):
    n = pl.program_id(1)

    @pl.when(n < n_cc)
    def _():
        for bb in range(bb_n):
            _attn_block(refs, bb, False, None, None)

    @pl.when(n >= n_cc)
    def _():
        for bb in range(bb_n):
            _attn_block(refs, bb, True, n == n_cc, n == n_cc + nb - 1)


def _attention(p, sink, att_cos, att_sin, att_cos_q, att_sin_q, n_ctx):
    b_, t_, _ = p.shape
    qb_ = 128
    n_cc = n_ctx // qb_
    nb = (t_ - n_ctx) // qb_
    d = D_MODEL
    bb_n = ATTN_BATCH_BLOCK if b_ % ATTN_BATCH_BLOCK == 0 else 1
    qblk, kblk, vblk = COL['att_q'] // 1024, COL['att_k'] // 256, COL['att_v'] // 256
    smem = pl.BlockSpec(memory_space=pltpu.SMEM)

    prev = lambda n: jnp.maximum(n - 1, n_cc)
    this = lambda n: jnp.maximum(n, n_cc)
    nxt = lambda n: jnp.maximum(jnp.minimum(n + 1, n_cc + nb - 1), n_cc)

    def tspec(width, blk, fn):
        return pl.BlockSpec((bb_n, qb_, width), lambda b, n: (b, fn(n), blk))

    def tab(fn):
        return pl.BlockSpec((qb_, ATT_HEAD_DIM), lambda b, n: (fn(n) - n_cc, 0))

    ctx_k = pl.BlockSpec((bb_n, n_ctx, 256), lambda b, n: (b, 0, kblk))
    ctx_v = pl.BlockSpec((bb_n, n_ctx, 256), lambda b, n: (b, 0, vblk))
    return pl.pallas_call(
        functools.partial(_attn_kernel, n_cc=n_cc, nb=nb, bb_n=bb_n),
        grid=(b_ // bb_n, n_cc + nb),
        in_specs=[smem, tspec(1024, qblk, lambda n: n),
                  tspec(256, kblk, prev), tspec(256, kblk, this), tspec(256, kblk, nxt),
                  tspec(256, vblk, prev), tspec(256, vblk, this), tspec(256, vblk, nxt),
                  ctx_k, ctx_v, tab(prev), tab(this), tab(nxt), tab(prev), tab(this), tab(nxt), tab(this), tab(this)],
        out_specs=pl.BlockSpec((bb_n, qb_, d), lambda b, n: (b, n, 0)),
        out_shape=jax.ShapeDtypeStruct((b_, t_, d), BF16),
        compiler_params=_cparams(("arbitrary", "arbitrary")),
        name="attention",
    )(sink, p, p, p, p, p, p, p, p, p, att_cos, att_cos, att_cos, att_sin, att_sin, att_sin, att_cos_q, att_sin_q)


def _merge_kernel(*refs, tm, n_half):
    (yg, ya, yr, pg, pa, pr, x_ref), mods = refs[:7], refs[7:7 + n_half]
    (g2_ref, wg, wa, wr, wo, wrh, wrl, br, xo_ref, h2_ref, route_ref, cnt_ref, cnt_scr) = refs[7 + n_half:]
    tile = tm * n_half

    @pl.when(pl.program_id(0) == 0)
    def _():
        cnt_scr[...] = jnp.zeros_like(cnt_scr)

    def mod_row(i):
        out = mods[0][i:i + 1, :]
        if n_half > 1:
            run = lax.broadcasted_iota(jnp.int32, (tile, 1), 0) // tm
            for a in range(1, n_half):
                out = jnp.where(run == a, mods[a][i:i + 1, :], out)
        return out

    merged = (_sigmoid(pg[...].astype(F32)) * _dot(yg[...], wg[...])
              + _sigmoid(pa[...].astype(F32)) * _dot(ya[...], wa[...])
              + _sigmoid(pr[...].astype(F32)) * _dot(yr[...], wr[...]))
    mix = _dot(merged.astype(BF16), wo[...])
    x = x_ref[...] + mod_row(2) * mix
    xo_ref[...] = x
    y = x * lax.rsqrt(jnp.mean(x * x, axis=-1, keepdims=True) + NORM_EPS) * g2_ref[...]
    h2 = y * (1.0 + mod_row(4)) + mod_row(3)
    h2_ref[...] = h2

    h_hi = h2.astype(BF16)
    h_lo = (h2 - h_hi.astype(F32)).astype(BF16)
    logits = _dot(h_hi, wrh[...]) + _dot(h_lo, wrh[...]) + _dot(h_hi, wrl[...]) + br[...]

    lane = lax.broadcasted_iota(jnp.int32, (tile, LANES), 1)
    lanef = lane.astype(F32)
    is_grp = lane < MOE_GROUPS
    gl = jnp.where(is_grp, logits, NEG)
    gmax = jnp.max(gl, axis=-1, keepdims=True)
    gidx = jnp.min(jnp.where(gl == gmax, lanef, float(LANES)), axis=-1, keepdims=True)
    gprob = 1.0 / jnp.sum(jnp.where(is_grp, jnp.exp(gl - gmax), 0.0), axis=-1, keepdims=True)
    lo = EXPERT_LANE0 + MOE_EPG * gidx
    el = jnp.where(jnp.logical_and(lanef >= lo, lanef < lo + MOE_EPG), logits, NEG)
    t1 = jnp.max(el, axis=-1, keepdims=True)
    i1 = jnp.min(jnp.where(el == t1, lanef, float(LANES)), axis=-1, keepdims=True)
    el2 = jnp.where(lanef == i1, NEG, el)
    t2 = jnp.max(el2, axis=-1, keepdims=True)
    i2 = jnp.min(jnp.where(el2 == t2, lanef, float(LANES)), axis=-1, keepdims=True)
    ex = jnp.exp(t2 - t1)
    w0 = gprob / (1.0 + ex)
    w1 = gprob * ex / (1.0 + ex)

    oh0 = lanef == i1
    oh1 = lanef == i2
    both = jnp.where(jnp.logical_or(oh0, oh1), 1.0, 0.0)
    rr = lax.broadcasted_iota(jnp.int32, (tile, tile), 0)
    cc = lax.broadcasted_iota(jnp.int32, (tile, tile), 1)
    tri = jnp.where(cc < rr, 1.0, 0.0).astype(BF16)
    base = cnt_scr[...] + _dot(tri, both.astype(BF16))
    r0 = jnp.sum(jnp.where(oh0, base, 0.0), axis=-1, keepdims=True)
    r1 = jnp.sum(jnp.where(oh1, base, 0.0), axis=-1, keepdims=True)
    cnt = cnt_scr[...] + jnp.sum(both, axis=0, keepdims=True)
    cnt_scr[...] = cnt
    cnt_ref[...] = jnp.broadcast_to(cnt, cnt_ref.shape)

    out = jnp.where(lane == 0, i1 - EXPERT_LANE0, 0.0)
    out = jnp.where(lane == 1, i2 - EXPERT_LANE0, out)
    out = jnp.where(lane == 2, w0, out)
    out = jnp.where(lane == 3, w1, out)
    out = jnp.where(lane == 4, r0, out)
    out = jnp.where(lane == 5, r1, out)
    route_ref[...] = out


def _merge(yg, ya, yr, p, xs, mod, g2, wg, wa, wr, wo, wrh, wrl, br, n_ctx):
    b_, t_, d = xs.shape
    n_tok = b_ * t_
    tm = 256
    nt = t_ // tm
    n_ct = n_ctx // tm
    n_half = 2 if n_tok % (2 * tm) == 0 else 1
    tile = n_half * tm
    gblk = COL['gates'] // d
    tok = pl.BlockSpec((tile, d), lambda g: (g, 0))
    wspec = pl.BlockSpec((d, d), lambda g: (0, 0))
    rspec = pl.BlockSpec((d, LANES), lambda g: (0, 0))

    def gate(i):
        return pl.BlockSpec((tile, d), lambda g: (g, gblk + i))

    def mspec(a):
        def imap(g):
            run = g * n_half + a
            return run // nt, jnp.where(run % nt < n_ct, 0, 1), 0, 0
        return pl.BlockSpec((None, None, 6, d), imap)

    flat = lambda arr: arr.reshape(n_tok, arr.shape[-1])
    xo, h2, route, counts = pl.pallas_call(
        functools.partial(_merge_kernel, tm=tm, n_half=n_half),
        grid=(n_tok // tile,),
        in_specs=[tok, tok, tok, gate(0), gate(1), gate(2), tok] + [mspec(a) for a in range(n_half)]
        + [pl.BlockSpec((1, d), lambda g: (0, 0)), wspec, wspec, wspec, wspec, rspec, rspec,
           pl.BlockSpec((1, LANES), lambda g: (0, 0))],
        out_specs=[tok, tok, pl.BlockSpec((tile, LANES), lambda g: (g, 0)),
                   pl.BlockSpec((8, LANES), lambda g: (0, 0))],
        out_shape=[jax.ShapeDtypeStruct((n_tok, d), F32), jax.ShapeDtypeStruct((n_tok, d), F32),
                   jax.ShapeDtypeStruct((n_tok, LANES), F32), jax.ShapeDtypeStruct((8, LANES), F32)],
        scratch_shapes=[pltpu.VMEM((1, LANES), F32)],
        compiler_params=_cparams(("arbitrary",)),
        name="merge_route",
    )(flat(yg), flat(ya), flat(yr), flat(p), flat(p), flat(p), flat(xs), *([mod] * n_half),
      g2.reshape(1, d), wg, wa, wr, wo, wrh, wrl, br)
    return xo.reshape(b_, t_, d), h2.reshape(b_, t_, d), route.reshape(b_, t_, LANES), counts


ROW_DMA_UNROLL = 16


def _issue_rows(row_copy, tm):
    def issue(c, carry):
        for u in range(ROW_DMA_UNROLL):
            row_copy(c * ROW_DMA_UNROLL + u, 0).start()
            row_copy(c * ROW_DMA_UNROLL + u, 1).start()
        return carry

    lax.fori_loop(0, tm // ROW_DMA_UNROLL, issue, 0)


def _drain_rows(row_copy, tm):
    def drain(c, carry):
        for u in range(ROW_DMA_UNROLL):
            row_copy(c * ROW_DMA_UNROLL + u, 0).wait()
            row_copy(c * ROW_DMA_UNROLL + u, 1).wait()
        return carry

    lax.fori_loop(0, tm // ROW_DMA_UNROLL, drain, 0)


def _dispatch_kernel(plan_ref, dest_ref, h_ref, xs_ref, zbuf, sem, zsem, *, tm, n_blocks):
    def zero_block(row0):
        return pltpu.make_async_copy(zbuf, xs_ref.at[pl.ds(pl.multiple_of(row0, MOE_BLOCK), MOE_BLOCK)], zsem)

    @pl.when(jnp.logical_and(pl.program_id(0) == 0, pl.program_id(1) == 0))
    def _():
        zbuf[...] = jnp.zeros_like(zbuf)

        def region_tails(fn):
            for e in range(MOE_EXPERTS):
                @pl.when(plan_ref[MOE_EXPERTS + e] > 0)
                def _():
                    fn(zero_block(plan_ref[e] - MOE_BLOCK))

        def trailing(fn):
            def body(i, carry):
                fn(zero_block(i * MOE_BLOCK))
                return carry
            lax.fori_loop(plan_ref[2 * MOE_EXPERTS], n_blocks, body, 0)

        region_tails(lambda cp: cp.start())
        trailing(lambda cp: cp.start())
        region_tails(lambda cp: cp.wait())
        trailing(lambda cp: cp.wait())

    def row_copy(i, k):
        return pltpu.make_async_copy(h_ref.at[pl.ds(i, 1)], xs_ref.at[pl.ds(dest_ref[0, k * tm + i], 1)], sem)

    _issue_rows(row_copy, tm)
    _drain_rows(row_copy, tm)


def _dispatch(h2, dest, plan, n_slots):
    b_, t_, d = h2.shape
    tm = 256
    nt = t_ // tm
    grid_spec = pltpu.PrefetchScalarGridSpec(
        num_scalar_prefetch=1,
        grid=(b_, nt),
        in_specs=[pl.BlockSpec((None, 1, 2 * tm), lambda b, t, plan: (b * nt + t, 0, 0), memory_space=pltpu.SMEM),
                  pl.BlockSpec((tm, d), lambda b, t, plan: (b * nt + t, 0))],
        out_specs=pl.BlockSpec(memory_space=pl.ANY),
        scratch_shapes=[pltpu.VMEM((MOE_BLOCK, d), F32), pltpu.SemaphoreType.DMA, pltpu.SemaphoreType.DMA],
    )
    return pl.pallas_call(
        functools.partial(_dispatch_kernel, tm=tm, n_blocks=n_slots // MOE_BLOCK),
        grid_spec=grid_spec,
        out_shape=jax.ShapeDtypeStruct((n_slots, d), F32),
        compiler_params=_cparams(("arbitrary", "arbitrary")),
        name="moe_dispatch",
    )(plan, dest, h2.reshape(b_ * t_, d))


def _expert_kernel(be_ref, nu_ref, x_ref, w1_ref, w3_ref, w2_ref, y_ref, wb):
    i = pl.program_id(0)
    used = i < nu_ref[0]
    new_expert = jnp.logical_or(i == 0, be_ref[i] != be_ref[jnp.maximum(i - 1, 0)])

    @pl.when(jnp.logical_and(used, new_expert))
    def _():
        wb[0] = w1_ref[...].astype(BF16)
        wb[1] = w3_ref[...].astype(BF16)
        wb[2] = w2_ref[...].astype(BF16)

    @pl.when(used)
    def _():
        xb = x_ref[...].astype(BF16)
        h1 = _dot(xb, wb[0])
        h3 = _dot(xb, wb[1])
        y_ref[...] = _dot((_silu(h1) * h3).astype(BF16), wb[2])

    @pl.when(i >= nu_ref[0])
    def _():
        y_ref[...] = jnp.zeros_like(y_ref)


def _experts(x_sorted, blk_exp, n_used, w1, w3, w2, layer):
    n_slots, d = x_sorted.shape
    hid = w1.shape[-1]
    n_blocks = n_slots // MOE_BLOCK
    grid_spec = pltpu.PrefetchScalarGridSpec(
        num_scalar_prefetch=2,
        grid=(n_blocks,),
        in_specs=[pl.BlockSpec((MOE_BLOCK, d), lambda i, be, nu: (i, 0)),
                  pl.BlockSpec((None, None, d, hid), lambda i, be, nu: (layer, be[i], 0, 0)),
                  pl.BlockSpec((None, None, d, hid), lambda i, be, nu: (layer, be[i], 0, 0)),
                  pl.BlockSpec((None, None, hid, d), lambda i, be, nu: (layer, be[i], 0, 0))],
        out_specs=pl.BlockSpec((MOE_BLOCK, d), lambda i, be, nu: (i, 0)),
        scratch_shapes=[pltpu.VMEM((3, d, hid), BF16)],
    )
    assert d == hid
    return pl.pallas_call(
        _expert_kernel,
        grid_spec=grid_spec,
        out_shape=jax.ShapeDtypeStruct((n_slots, d), F32),
        compiler_params=_cparams(("arbitrary",)),
        name="moe_experts",
    )(blk_exp, n_used, x_sorted, w1, w3, w2)


def _combine_kernel(dcur_ref, dnext_ref, x_ref, route_ref, mod_ref, fg_ref, y_ref, o_ref, buf, sem, *,
                    tm, final, n_steps):
    g = pl.program_id(0)
    slot = g % 2

    def row_copy(dest_ref, sl):
        def make(i, k):
            return pltpu.make_async_copy(y_ref.at[pl.ds(dest_ref[0, k * tm + i], 1)],
                                         buf.at[sl, k, pl.ds(i, 1)], sem.at[sl])
        return make

    @pl.when(g == 0)
    def _():
        _issue_rows(row_copy(dcur_ref, 0), tm)

    @pl.when(g + 1 < n_steps)
    def _():
        _issue_rows(row_copy(dnext_ref, 1 - slot), tm)

    _drain_rows(row_copy(dcur_ref, slot), tm)

    r = route_ref[...]
    f = buf[slot, 0] * r[:, 2:3] + buf[slot, 1] * r[:, 3:4]
    x = x_ref[...] + mod_ref[5:6, :] * f
    if final:
        x = x * lax.rsqrt(jnp.mean(x * x, axis=-1, keepdims=True) + NORM_EPS) * fg_ref[...]
    o_ref[...] = x


def _combine(xs, route, mod, final_g, y_sorted, dest, n_ctx, final):
    b_, t_, d = xs.shape
    tm = 256
    nt = t_ // tm
    n_ct = n_ctx // tm
    skip = n_ct if final else 0
    ntl = nt - skip
    n_steps = b_ * ntl

    def bt(g):
        return g // ntl, g % ntl + skip

    def tok(w):
        return pl.BlockSpec((None, tm, w), lambda g: (*bt(g), 0))

    def dspec(off):
        def imap(g):
            b, t = bt(jnp.minimum(g + off, n_steps - 1))
            return b * nt + t, 0, 0
        return pl.BlockSpec((None, 1, 2 * tm), imap, memory_space=pltpu.SMEM)

    return pl.pallas_call(
        functools.partial(_combine_kernel, tm=tm, final=final, n_steps=n_steps),
        grid=(n_steps,),
        in_specs=[dspec(0), dspec(1), tok(d), tok(LANES),
                  pl.BlockSpec((None, None, 6, d), lambda g: (g // ntl, jnp.where(g % ntl + skip < n_ct, 0, 1), 0, 0)),
                  pl.BlockSpec((1, d), lambda g: (0, 0)),
                  pl.BlockSpec(memory_space=pl.ANY)],
        out_specs=pl.BlockSpec((None, tm, d), lambda g: (g // ntl, g % ntl, 0)),
        out_shape=jax.ShapeDtypeStruct((b_, t_ - skip * tm, d), F32),
        scratch_shapes=[pltpu.VMEM((2, 2, tm, d), F32), pltpu.SemaphoreType.DMA((2,))],
        compiler_params=_cparams(("arbitrary",)),
        name="moe_combine",
    )(dest, dest, xs, route, mod, final_g.reshape(1, d), y_sorted)


def _slot_plan(route, counts, tm):
    b_, t_, _ = route.shape
    n_tok = b_ * t_
    cnt = counts[0, EXPERT_LANE0:EXPERT_LANE0 + MOE_EXPERTS].astype(jnp.int32)
    padded = (cnt + MOE_BLOCK - 1) // MOE_BLOCK * MOE_BLOCK
    pad_end = jnp.cumsum(padded)
    pad_start = pad_end - padded
    n_blocks = (n_tok * 2 + MOE_EXPERTS * (MOE_BLOCK - 1) + MOE_BLOCK - 1) // MOE_BLOCK
    n_used = (pad_end[-1] // MOE_BLOCK).astype(jnp.int32)
    blk = jnp.minimum(jnp.arange(n_blocks, dtype=jnp.int32), n_used - 1) * MOE_BLOCK
    blk_exp = jnp.sum((pad_end[None, :] <= blk[:, None]).astype(jnp.int32), axis=1)
    blk_exp = jnp.minimum(blk_exp, MOE_EXPERTS - 1).astype(jnp.int32)
    e = route[..., 0:2].astype(jnp.int32)
    rank = route[..., 4:6].astype(jnp.int32)
    eid = jnp.arange(MOE_EXPERTS, dtype=jnp.int32)
    dest = rank + jnp.sum(jnp.where(e[..., None] == eid, pad_start, 0), axis=-1)
    dest = dest.reshape(b_ * t_ // tm, tm, 2).transpose(0, 2, 1).reshape(b_ * t_ // tm, 1, 2 * tm)
    plan = jnp.concatenate([pad_end, padded, n_used.reshape(1)]).astype(jnp.int32)
    return dest, blk_exp, n_used.reshape(1), plan, n_blocks * MOE_BLOCK


def _permute_w_in(w_in):
    parts = [w_in[:, _REF_OFF[name][0]:_REF_OFF[name][0] + _REF_OFF[name][1]] for name in _OUR_ORDER]
    width = sum(part.shape[1] for part in parts)
    parts.append(jnp.zeros((w_in.shape[0], N_P - width), w_in.dtype))
    return jnp.concatenate(parts, axis=1).astype(BF16)


def _rope_tables(n_ctx, l_):
    half = SCAN_DK // 2
    inv = ROPE_BASE ** (-jnp.arange(half, dtype=F32) / half)
    ang = jnp.arange(n_ctx + l_, dtype=F32)[:, None] * inv[None, :]
    ret_cos = jnp.concatenate([jnp.cos(ang), jnp.cos(ang)], axis=-1)
    ret_sin = jnp.concatenate([-jnp.sin(ang), jnp.sin(ang)], axis=-1)
    q = ATT_HEAD_DIM // 4
    inv = ROPE_BASE ** (-jnp.arange(q, dtype=F32) / q)
    tpos = jnp.arange(l_)
    a_r = (tpos // GRID_W).astype(F32)[:, None] * inv[None, :]
    a_c = (tpos % GRID_W).astype(F32)[:, None] * inv[None, :]
    att_cos = jnp.concatenate([jnp.cos(a_r), jnp.cos(a_r), jnp.cos(a_c), jnp.cos(a_c)], axis=-1)
    att_sin = jnp.concatenate([-jnp.sin(a_r), jnp.sin(a_r), -jnp.sin(a_c), jnp.sin(a_c)], axis=-1)
    q_scale = ATT_HEAD_DIM ** -0.5
    return ret_cos, ret_sin, att_cos, att_sin, att_cos * q_scale, att_sin * q_scale


def kernel(x, c, ctx, c_ctx, w_ada, b_ada, norm1_g, norm2_g, w_in, gla_wa2, gla_ba, gla_norm_g, attn_sink,
           ret_norm_g, w_br_gla, w_br_attn, w_br_ret, w_out, moe_w_grp, moe_b_grp, moe_w_exp, moe_b_exp,
           moe_w1, moe_w3, moe_w2, final_g):
    b_, l_, d = x.shape
    n_ctx = ctx.shape[1]
    depth = w_ada.shape[0]
    assert d == D_MODEL and n_ctx % 256 == 0 and l_ % 256 == 0 and b_ <= 8

    xs = jnp.concatenate([ctx, x], axis=1)
    cc = jnp.zeros((16, d), F32).at[:b_].set(c).at[b_].set(c_ctx)
    mod_all = _modulation(cc, w_ada, b_ada)
    ret_cos, ret_sin, att_cos, att_sin, att_cos_q, att_sin_q = _rope_tables(n_ctx, l_)

    for layer in range(depth):
        last = layer == depth - 1
        m = mod_all[layer].reshape(16, 6, d)
        mod = jnp.stack([jnp.broadcast_to(m[b_], (b_, 6, d)), m[:b_]], axis=1)

        w_p = _permute_w_in(w_in[layer])
        wa2_pad = jnp.zeros((2, LANES, 512), F32)
        wa2_pad = wa2_pad.at[0, 0:GLA_LOW_RANK].set(gla_wa2[layer, 0])
        wa2_pad = wa2_pad.at[1, GLA_LOW_RANK:2 * GLA_LOW_RANK].set(gla_wa2[layer, 1]).astype(BF16)
        ba = gla_ba[layer].reshape(2, 1, 512)

        p = _in_projection(xs, mod, norm1_g[layer], w_p, n_ctx)
        obg, obr = _scan(p, ret_cos, ret_sin, wa2_pad, ba, n_ctx, True)
        yg, yr = _scan(p, ret_cos, ret_sin, wa2_pad, ba, n_ctx, False,
                       (obg, obr, gla_norm_g[layer], ret_norm_g[layer]))
        ya = _attention(p, attn_sink[layer], att_cos, att_sin, att_cos_q, att_sin_q, n_ctx)

        w_route = jnp.zeros((d, LANES), F32)
        w_route = w_route.at[:, 0:MOE_GROUPS].set(moe_w_grp[layer])
        w_route = w_route.at[:, EXPERT_LANE0:EXPERT_LANE0 + MOE_EXPERTS].set(moe_w_exp[layer])
        wr_hi = w_route.astype(BF16)
        wr_lo = (w_route - wr_hi.astype(F32)).astype(BF16)
        b_route = jnp.zeros((1, LANES), F32)
        b_route = b_route.at[0, 0:MOE_GROUPS].set(moe_b_grp[layer])
        b_route = b_route.at[0, EXPERT_LANE0:EXPERT_LANE0 + MOE_EXPERTS].set(moe_b_exp[layer])

        xs, h2, route, counts = _merge(
            yg, ya, yr, p, xs, mod, norm2_g[layer],
            w_br_gla[layer].astype(BF16), w_br_attn[layer].astype(BF16), w_br_ret[layer].astype(BF16),
            w_out[layer].astype(BF16), wr_hi, wr_lo, b_route, n_ctx)

        dest, blk_exp, n_used, plan, n_slots = _slot_plan(route, counts, 256)
        x_sorted = _dispatch(h2, dest, plan, n_slots)
        y_sorted = _experts(x_sorted, blk_exp, n_used, moe_w1, moe_w3, moe_w2, layer)
        xs = _combine(xs, route, mod, final_g, y_sorted, dest, n_ctx, last)

    return xs
```

```python
import functools
import math

import jax
import jax.numpy as jnp
from jax import lax
from jax.experimental import pallas as pl
from jax.experimental.pallas import tpu as pltpu

F32 = jnp.float32
BF16 = jnp.bfloat16

D_MODEL = 1024
GRID_W = 64
CHUNK = 128
NORM_EPS = 1e-6
ROPE_BASE = 10000.0

SCAN_HEADS = 4
SCAN_DK = 128
SCAN_DV = 256
GLA_LOW_RANK = 16
GLA_TAU = 16.0

ATT_HEAD_DIM = 128
ATT_Q_HEADS = 8
ATT_KV_HEADS = 2
ATT_GROUP = ATT_Q_HEADS // ATT_KV_HEADS

MOE_GROUPS = 4
MOE_EPG = 8
MOE_EXPERTS = 32
MOE_BLOCK = 512
EXPERT_LANE0 = 32

LANES = 128
INPROJ_TN = 2816
NEG = -1e30

_REF_LAYOUT = (
    ('gla_q', 512), ('gla_k', 512), ('gla_v', 1024), ('gla_r', 1024), ('gla_lr', 32),
    ('att_q', 1024), ('att_k', 256), ('att_v', 256),
    ('ret_q', 512), ('ret_k', 512), ('ret_v', 1024), ('ret_g', 1024), ('gates', 3072),
)
_OUR_ORDER = ('gla_v', 'gla_r', 'att_q', 'ret_v', 'ret_g', 'gates', 'gla_q', 'gla_k', 'ret_q', 'ret_k',
              'att_k', 'att_v', 'gla_lr')


def _layout():
    ref_off, start = {}, 0
    for name, width in _REF_LAYOUT:
        ref_off[name] = (start, width)
        start += width
    col, off = {}, 0
    for name in _OUR_ORDER:
        col[name] = off
        off += ref_off[name][1]
    n_p = (off + INPROJ_TN - 1) // INPROJ_TN * INPROJ_TN
    return ref_off, col, n_p


_REF_OFF, COL, N_P = _layout()
VMEM_LIMIT = 56 * 1024 * 1024


def _cparams(sem):
    return pltpu.CompilerParams(dimension_semantics=sem, vmem_limit_bytes=VMEM_LIMIT)


def _silu(x):
    return x / (1.0 + jnp.exp(-x))


def _sigmoid(x):
    return 1.0 / (1.0 + jnp.exp(-x))


def _dot(a, b):
    return jnp.dot(a, b, preferred_element_type=F32)


def _dot_nt(a, b):
    return lax.dot_general(a, b, (((1,), (1,)), ((), ())), preferred_element_type=F32)


def _dot_tn(a, b):
    return lax.dot_general(a, b, (((0,), (0,)), ((), ())), preferred_element_type=F32)


def _mod_kernel(c_ref, w_ref, b_ref, o_ref):
    s = _silu(c_ref[...]).astype(BF16)
    o_ref[...] = _dot(s, w_ref[...].astype(BF16)) + b_ref[...]


def _modulation(cc, w_ada, b_ada):
    depth, d, n6 = w_ada.shape
    rows = cc.shape[0]
    tn = 512
    return pl.pallas_call(
        _mod_kernel,
        grid=(depth, n6 // tn),
        in_specs=[pl.BlockSpec((rows, d), lambda l, j: (0, 0)),
                  pl.BlockSpec((None, d, tn), lambda l, j: (l, 0, j)),
                  pl.BlockSpec((None, 1, tn), lambda l, j: (l, 0, j))],
        out_specs=pl.BlockSpec((None, rows, tn), lambda l, j: (l, 0, j)),
        out_shape=jax.ShapeDtypeStruct((depth, rows, n6), F32),
        compiler_params=_cparams(("arbitrary", "arbitrary")),
        name="modulation",
    )(cc, w_ada, b_ada.reshape(depth, 1, n6))


def _inproj_kernel(x_ref, mod_ref, g_ref, w_ref, o_ref, h_scr, *, n_ctx, tm):
    t = pl.program_id(1)

    @pl.when(pl.program_id(2) == 0)
    def _():
        x = x_ref[...]
        y = x * lax.rsqrt(jnp.mean(x * x, axis=-1, keepdims=True) + NORM_EPS) * g_ref[...]
        row = t * tm + lax.broadcasted_iota(jnp.int32, (tm, 1), 0)
        is_ctx = row < n_ctx
        shift = jnp.where(is_ctx, mod_ref[0, 0:1, :], mod_ref[1, 0:1, :])
        scale = jnp.where(is_ctx, mod_ref[0, 1:2, :], mod_ref[1, 1:2, :])
        h_scr[...] = (y * (1.0 + scale) + shift).astype(BF16)

    o_ref[...] = _dot(h_scr[...], w_ref[...]).astype(o_ref.dtype)


def _in_projection(xs, mod, g, w_p, n_ctx):
    b_, t_, d = xs.shape
    tm = 1408 if t_ % 1408 == 0 else (768 if t_ % 768 == 0 else 256)
    tn = INPROJ_TN
    return pl.pallas_call(
        functools.partial(_inproj_kernel, n_ctx=n_ctx, tm=tm),
        grid=(b_, t_ // tm, N_P // tn),
        in_specs=[pl.BlockSpec((None, tm, d), lambda b, t, j: (b, t, 0)),
                  pl.BlockSpec((None, 2, 6, d), lambda b, t, j: (b, 0, 0, 0)),
                  pl.BlockSpec((1, d), lambda b, t, j: (0, 0)),
                  pl.BlockSpec((d, tn), lambda b, t, j: (0, j))],
        out_specs=pl.BlockSpec((None, tm, tn), lambda b, t, j: (b, t, j)),
        out_shape=jax.ShapeDtypeStruct((b_, t_, N_P), BF16),
        scratch_shapes=[pltpu.VMEM((tm, d), BF16)],
        compiler_params=_cparams(("arbitrary", "arbitrary", "arbitrary")),
        name="in_projection",
    )(xs, mod, g.reshape(1, d), w_p)


def _chunk_update(q_in, k_in, k_out, v_bf, dec, s_ref, idx, mask):
    qb = q_in.astype(BF16)
    sc = jnp.where(mask, _dot_nt(qb, k_in.astype(BF16)), 0.0)
    st = s_ref[idx]
    o = _dot(sc.astype(BF16), v_bf) + _dot_nt(qb, st.astype(BF16))
    s_ref[idx] = st * dec + _dot_tn(v_bf, k_out.astype(BF16))
    return o


def _split3(x):
    hi = x.astype(BF16)
    r = x - hi.astype(F32)
    mid = r.astype(BF16)
    lo = (r - mid.astype(F32)).astype(BF16)
    return hi, mid, lo


def _group_norm_gate(o, g, r):
    mu = jnp.mean(o, axis=-1, keepdims=True)
    oc = o - mu
    var = jnp.mean(oc * oc, axis=-1, keepdims=True)
    return oc * lax.rsqrt(var + NORM_EPS) * g * _silu(r)


SCAN_BATCH_BLOCK = 4
SCAN_BATCH_BLOCK_BWD = 8


def _scan_kernel(*refs, backward, bb_n):
    if backward:
        (gq, gk, gv, lr, rq, rk, rv, cos, sin, wa2, ba, obg, obr, sg, sr) = refs
    else:
        (gq, gk, gv, lr, rq, rk, rv, cos, sin, wa2, ba, gr, rg, obg, obr, gng, rng, yg, yr, sg, sr) = refs
    c_ = CHUNK

    @pl.when(pl.program_id(1) == 0)
    def _():
        sg[...] = jnp.zeros_like(sg)
        sr[...] = jnp.zeros_like(sr)

    row = lax.broadcasted_iota(jnp.int32, (c_, c_), 0)
    col = lax.broadcasted_iota(jnp.int32, (c_, c_), 1)
    if backward:
        tri = jnp.where(col >= row, 1.0, 0.0).astype(BF16)
        mask = col > row
    else:
        tri = jnp.where(col <= row, 1.0, 0.0).astype(BF16)
        mask = col <= row
    last = 0 if backward else c_ - 1

    pos = lax.broadcasted_iota(jnp.int32, (c_, SCAN_DK), 0)
    steps = ((c_ - pos) if backward else (pos + 1)).astype(F32)
    cosv = cos[...]
    sinv = sin[...]
    scale = SCAN_DK ** -0.5

    for bb in range(bb_n):
        for h in range(SCAN_HEADS):
            ks = slice(h * SCAN_DK, (h + 1) * SCAN_DK)
            vs = slice(h * SCAN_DV, (h + 1) * SCAN_DV)
            ld = math.log(1.0 - 2.0 ** (-5.0 - h))
            c = steps * ld
            q = rq[bb, :, ks].astype(F32)
            k = rk[bb, :, ks].astype(F32)
            q = q * cosv + pltpu.roll(q, SCAN_DK // 2, 1) * sinv
            k = (k * cosv + pltpu.roll(k, SCAN_DK // 2, 1) * sinv) * scale
            dec = jnp.full((1, SCAN_DK), math.exp(c_ * ld), F32)
            k_in = k * jnp.exp(-c)
            o = _chunk_update(q * jnp.exp(c), k_in, k_in * math.exp(c_ * ld), rv[bb, :, vs], dec, sr, (bb, h), mask)
            if backward:
                obr[bb, :, vs] = o
            else:
                yr[bb, :, vs] = _group_norm_gate(o + obr[bb, :, vs], rng[:, vs],
                                                 rg[bb, :, vs].astype(F32)).astype(yr.dtype)

    for bb in range(bb_n):
        pre = _dot(lr[bb], wa2[...]) + ba[...]
        la = (jnp.minimum(pre, 0.0) - jnp.log1p(jnp.exp(-jnp.abs(pre)))) * (1.0 / GLA_TAU)
        hi, mid, lo = _split3(la)
        cum = _dot(tri, hi) + _dot(tri, mid) + _dot(tri, lo)
        tot = cum[last:last + 1, :]
        for h in range(SCAN_HEADS):
            ks = slice(h * SCAN_DK, (h + 1) * SCAN_DK)
            vs = slice(h * SCAN_DV, (h + 1) * SCAN_DV)
            c = cum[:, ks]
            th = tot[:, ks]
            q = gq[bb, :, ks].astype(F32) * scale
            k = gk[bb, :, ks].astype(F32)
            dec = jnp.exp(th)
            k_in = k * jnp.exp(-c)
            o = _chunk_update(q * jnp.exp(c), k_in, k_in * dec, gv[bb, :, vs], dec, sg, (bb, h), mask)
            if backward:
                obg[bb, :, vs] = o
            else:
                yg[bb, :, vs] = _group_norm_gate(o + obg[bb, :, vs], gng[:, vs],
                                                 gr[bb, :, vs].astype(F32)).astype(yg.dtype)


def _scan(p, ret_cos, ret_sin, wa2_pad, ba, n_ctx, backward, extras=None):
    b_, t_, _ = p.shape
    n_chunks = t_ // CHUNK
    n_cc = n_ctx // CHUNK
    d = D_MODEL
    want = SCAN_BATCH_BLOCK_BWD if backward else SCAN_BATCH_BLOCK
    bb_n = want if b_ % want == 0 else 1

    if backward:
        def cidx(s):
            return jnp.where(s < n_cc, n_cc - 1 - s, n_chunks - 1 + n_cc - s)
    else:
        def cidx(s):
            return s

    def pspec(name, width):
        blk = COL[name] // width
        return pl.BlockSpec((bb_n, CHUNK, width), lambda b, s: (b, cidx(s), blk))

    dirn = 1 if backward else 0
    tok = lambda width: pl.BlockSpec((bb_n, CHUNK, width), lambda b, s: (b, cidx(s), 0))
    in_specs = [pspec('gla_q', 512), pspec('gla_k', 512), pspec('gla_v', 1024), pspec('gla_lr', LANES),
                pspec('ret_q', 512), pspec('ret_k', 512), pspec('ret_v', 1024),
                pl.BlockSpec((CHUNK, SCAN_DK), lambda b, s: (cidx(s), 0)),
                pl.BlockSpec((CHUNK, SCAN_DK), lambda b, s: (cidx(s), 0)),
                pl.BlockSpec((None, LANES, 512), lambda b, s: (dirn, 0, 0)),
                pl.BlockSpec((None, 1, 512), lambda b, s: (dirn, 0, 0))]
    args = [p, p, p, p, p, p, p, ret_cos, ret_sin, wa2_pad, ba]
    state = [pltpu.VMEM((bb_n, SCAN_HEADS, SCAN_DV, SCAN_DK), F32),
             pltpu.VMEM((bb_n, SCAN_HEADS, SCAN_DV, SCAN_DK), F32)]
    if backward:
        out_specs = [tok(d), tok(d)]
        out_shape = [jax.ShapeDtypeStruct((b_, t_, d), F32)] * 2
    else:
        obg, obr, gng, rng = extras
        in_specs += [pspec('gla_r', 1024), pspec('ret_g', 1024), tok(d), tok(d),
                     pl.BlockSpec((1, d), lambda b, s: (0, 0)), pl.BlockSpec((1, d), lambda b, s: (0, 0))]
        args += [p, p, obg, obr, gng.reshape(1, d), rng.reshape(1, d)]
        out_specs = [tok(d), tok(d)]
        out_shape = [jax.ShapeDtypeStruct((b_, t_, d), BF16)] * 2
    return pl.pallas_call(
        functools.partial(_scan_kernel, backward=backward, bb_n=bb_n),
        grid=(b_ // bb_n, n_chunks),
        in_specs=in_specs, out_specs=out_specs, out_shape=out_shape,
        scratch_shapes=state,
        compiler_params=_cparams(("arbitrary", "arbitrary")),
        name="scan_bwd" if backward else "scan_fwd",
    )(*args)


def _rope_axial(x, cosv, sinv, lane_lo):
    partner = jnp.where(lane_lo, pltpu.roll(x, 96, 1), pltpu.roll(x, 32, 1))
    return x * cosv + partner * sinv


def _attn_block(refs, bb, local, first, last):
    (sink, q, kp, ks, kn, vp, vs, vn, kc, vc, cp, cs, cn, sp, ss, sn, cq, sq, o_ref) = refs
    qb_ = q.shape[1]
    hd = ATT_HEAD_DIM
    scale = hd ** -0.5
    rows = ATT_GROUP * qb_
    rowi = lax.broadcasted_iota(jnp.int32, (rows, 1), 0)
    if local:
        lane = lax.broadcasted_iota(jnp.int32, (qb_, hd), 1)
        lane_lo = (lane & 63) < 32
        i_idx = lax.broadcasted_iota(jnp.int32, (rows, qb_), 0) & (qb_ - 1)
        j_idx = lax.broadcasted_iota(jnp.int32, (rows, qb_), 1)
        mask_p = jnp.logical_and(j_idx >= i_idx, jnp.logical_not(first))
        mask_n = jnp.logical_and(j_idx <= i_idx, jnp.logical_not(last))

    for g in range(ATT_KV_HEADS):
        gs = slice(g * hd, (g + 1) * hd)
        heads = []
        for hh in range(ATT_GROUP):
            qs = slice((g * ATT_GROUP + hh) * hd, (g * ATT_GROUP + hh + 1) * hd)
            qh = q[bb, :, qs].astype(F32)
            qh = _rope_axial(qh, cq[...], sq[...], lane_lo) if local else qh * scale
            heads.append(qh.astype(BF16))
        q4 = jnp.concatenate(heads, axis=0)
        sink_col = jnp.full((rows, 1), sink[g * ATT_GROUP], F32)
        for hh in range(1, ATT_GROUP):
            sink_col = jnp.where(rowi >= hh * qb_, sink[g * ATT_GROUP + hh], sink_col)

        s_c = _dot_nt(q4, kc[bb, :, gs])
        n_c = s_c.shape[1] // qb_
        m_el = s_c[:, 0:qb_]
        for cpart in range(1, n_c):
            m_el = jnp.maximum(m_el, s_c[:, cpart * qb_:(cpart + 1) * qb_])
        if local:
            k_p = _rope_axial(kp[bb, :, gs].astype(F32), cp[...], sp[...], lane_lo).astype(BF16)
            k_s = _rope_axial(ks[bb, :, gs].astype(F32), cs[...], ss[...], lane_lo).astype(BF16)
            k_n = _rope_axial(kn[bb, :, gs].astype(F32), cn[...], sn[...], lane_lo).astype(BF16)
            s_p = jnp.where(mask_p, _dot_nt(q4, k_p), NEG)
            s_s = _dot_nt(q4, k_s)
            s_n = jnp.where(mask_n, _dot_nt(q4, k_n), NEG)
            m_el = jnp.maximum(jnp.maximum(m_el, s_p), jnp.maximum(s_s, s_n))
        m = jnp.maximum(sink_col, jnp.max(m_el, axis=-1, keepdims=True))
        p_c = jnp.exp(s_c - m)
        d_el = p_c[:, 0:qb_]
        for cpart in range(1, n_c):
            d_el = d_el + p_c[:, cpart * qb_:(cpart + 1) * qb_]
        acc = _dot(p_c.astype(BF16), vc[bb, :, gs])
        if local:
            p_p = jnp.exp(s_p - m)
            p_s = jnp.exp(s_s - m)
            p_n = jnp.exp(s_n - m)
            d_el = d_el + p_p + p_s + p_n
            acc = acc + _dot(p_p.astype(BF16), vp[bb, :, gs]) + _dot(p_s.astype(BF16), vs[bb, :, gs]) \
                + _dot(p_n.astype(BF16), vn[bb, :, gs])
        den = jnp.exp(sink_col - m) + jnp.sum(d_el, axis=-1, keepdims=True)
        out = acc / den
        for hh in range(ATT_GROUP):
            qs = slice((g * ATT_GROUP + hh) * hd, (g * ATT_GROUP + hh + 1) * hd)
            o_ref[bb, :, qs] = out[hh * qb_:(hh + 1) * qb_, :].astype(o_ref.dtype)


ATTN_BATCH_BLOCK = 4


def _attn_kernel(*refs, n_cc, nb, bb_n):
    n = pl.program_id(1)

    @pl.when(n < n_cc)
    def _():
        for bb in range(bb_n):
            _attn_block(refs, bb, False, None, None)

    @pl.when(n >= n_cc)
    def _():
        for bb in range(bb_n):
            _attn_block(refs, bb, True, n == n_cc, n == n_cc + nb - 1)


def _attention(p, sink, att_cos, att_sin, att_cos_q, att_sin_q, n_ctx):
    b_, t_, _ = p.shape
    qb_ = 128
    n_cc = n_ctx // qb_
    nb = (t_ - n_ctx) // qb_
    d = D_MODEL
    bb_n = ATTN_BATCH_BLOCK if b_ % ATTN_BATCH_BLOCK == 0 else 1
    qblk, kblk, vblk = COL['att_q'] // 1024, COL['att_k'] // 256, COL['att_v'] // 256
    smem = pl.BlockSpec(memory_space=pltpu.SMEM)

    prev = lambda n: jnp.maximum(n - 1, n_cc)
    this = lambda n: jnp.maximum(n, n_cc)
    nxt = lambda n: jnp.maximum(jnp.minimum(n + 1, n_cc + nb - 1), n_cc)

    def tspec(width, blk, fn):
        return pl.BlockSpec((bb_n, qb_, width), lambda b, n: (b, fn(n), blk))

    def tab(fn):
        return pl.BlockSpec((qb_, ATT_HEAD_DIM), lambda b, n: (fn(n) - n_cc, 0))

    ctx_k = pl.BlockSpec((bb_n, n_ctx, 256), lambda b, n: (b, 0, kblk))
    ctx_v = pl.BlockSpec((bb_n, n_ctx, 256), lambda b, n: (b, 0, vblk))
    return pl.pallas_call(
        functools.partial(_attn_kernel, n_cc=n_cc, nb=nb, bb_n=bb_n),
        grid=(b_ // bb_n, n_cc + nb),
        in_specs=[smem, tspec(1024, qblk, lambda n: n),
                  tspec(256, kblk, prev), tspec(256, kblk, this), tspec(256, kblk, nxt),
                  tspec(256, vblk, prev), tspec(256, vblk, this), tspec(256, vblk, nxt),
                  ctx_k, ctx_v, tab(prev), tab(this), tab(nxt), tab(prev), tab(this), tab(nxt), tab(this), tab(this)],
        out_specs=pl.BlockSpec((bb_n, qb_, d), lambda b, n: (b, n, 0)),
        out_shape=jax.ShapeDtypeStruct((b_, t_, d), BF16),
        compiler_params=_cparams(("arbitrary", "arbitrary")),
        name="attention",
    )(sink, p, p, p, p, p, p, p, p, p, att_cos, att_cos, att_cos, att_sin, att_sin, att_sin, att_cos_q, att_sin_q)


def _merge_kernel(*refs, tm, n_half):
    (yg, ya, yr, pg, pa, pr, x_ref), mods = refs[:7], refs[7:7 + n_half]
    (g2_ref, wg, wa, wr, wo, wrh, wrl, br, xo_ref, h2_ref, route_ref, cnt_ref, cnt_scr) = refs[7 + n_half:]
    tile = tm * n_half

    @pl.when(pl.program_id(0) == 0)
    def _():
        cnt_scr[...] = jnp.zeros_like(cnt_scr)

    def mod_row(i):
        out = mods[0][i:i + 1, :]
        if n_half > 1:
            run = lax.broadcasted_iota(jnp.int32, (tile, 1), 0) // tm
            for a in range(1, n_half):
                out = jnp.where(run == a, mods[a][i:i + 1, :], out)
        return out

    merged = (_sigmoid(pg[...].astype(F32)) * _dot(yg[...], wg[...])
              + _sigmoid(pa[...].astype(F32)) * _dot(ya[...], wa[...])
              + _sigmoid(pr[...].astype(F32)) * _dot(yr[...], wr[...]))
    mix = _dot(merged.astype(BF16), wo[...])
    x = x_ref[...] + mod_row(2) * mix
    xo_ref[...] = x
    y = x * lax.rsqrt(jnp.mean(x * x, axis=-1, keepdims=True) + NORM_EPS) * g2_ref[...]
    h2 = y * (1.0 + mod_row(4)) + mod_row(3)
    h2_ref[...] = h2

    h_hi = h2.astype(BF16)
    h_lo = (h2 - h_hi.astype(F32)).astype(BF16)
    both_w = _dot(h_hi, wrl[...])
    logits = both_w[:, :LANES] + both_w[:, LANES:] + _dot(h_lo, wrh[...]) + br[...]

    lane = lax.broadcasted_iota(jnp.int32, (tile, LANES), 1)
    lanef = lane.astype(F32)
    is_grp = lane < MOE_GROUPS
    gl = jnp.where(is_grp, logits, NEG)
    gmax = jnp.max(gl, axis=-1, keepdims=True)
    gidx = jnp.min(jnp.where(gl == gmax, lanef, float(LANES)), axis=-1, keepdims=True)
    gprob = 1.0 / jnp.sum(jnp.where(is_grp, jnp.exp(gl - gmax), 0.0), axis=-1, keepdims=True)
    lo = EXPERT_LANE0 + MOE_EPG * gidx
    el = jnp.where(jnp.logical_and(lanef >= lo, lanef < lo + MOE_EPG), logits, NEG)
    t1 = jnp.max(el, axis=-1, keepdims=True)
    i1 = jnp.min(jnp.where(el == t1, lanef, float(LANES)), axis=-1, keepdims=True)
    el2 = jnp.where(lanef == i1, NEG, el)
    t2 = jnp.max(el2, axis=-1, keepdims=True)
    i2 = jnp.min(jnp.where(el2 == t2, lanef, float(LANES)), axis=-1, keepdims=True)
    ex = jnp.exp(t2 - t1)
    w0 = gprob / (1.0 + ex)
    w1 = gprob * ex / (1.0 + ex)

    oh0 = lanef == i1
    oh1 = lanef == i2
    both = jnp.where(jnp.logical_or(oh0, oh1), 1.0, 0.0)
    rr = lax.broadcasted_iota(jnp.int32, (tile, tile), 0)
    cc = lax.broadcasted_iota(jnp.int32, (tile, tile), 1)
    tri = jnp.where(cc < rr, 1.0, 0.0).astype(BF16)
    base = cnt_scr[...] + _dot(tri, both.astype(BF16))
    r0 = jnp.sum(jnp.where(oh0, base, 0.0), axis=-1, keepdims=True)
    r1 = jnp.sum(jnp.where(oh1, base, 0.0), axis=-1, keepdims=True)
    cnt = cnt_scr[...] + jnp.sum(both, axis=0, keepdims=True)
    cnt_scr[...] = cnt
    cnt_ref[...] = jnp.broadcast_to(cnt, cnt_ref.shape)

    out = jnp.where(lane == 0, i1 - EXPERT_LANE0, 0.0)
    out = jnp.where(lane == 1, i2 - EXPERT_LANE0, out)
    out = jnp.where(lane == 2, w0, out)
    out = jnp.where(lane == 3, w1, out)
    out = jnp.where(lane == 4, r0, out)
    out = jnp.where(lane == 5, r1, out)
    route_ref[...] = out


def _merge(yg, ya, yr, p, xs, mod, g2, wg, wa, wr, wo, wrh, wrl, br, n_ctx):
    b_, t_, d = xs.shape
    n_tok = b_ * t_
    tm = 256
    nt = t_ // tm
    n_ct = n_ctx // tm
    n_half = 2 if n_tok % (2 * tm) == 0 else 1
    tile = n_half * tm
    gblk = COL['gates'] // d
    tok = pl.BlockSpec((tile, d), lambda g: (g, 0))
    wspec = pl.BlockSpec((d, d), lambda g: (0, 0))
    rspec = pl.BlockSpec((d, LANES), lambda g: (0, 0))
    rspec2 = pl.BlockSpec((d, 2 * LANES), lambda g: (0, 0))

    def gate(i):
        return pl.BlockSpec((tile, d), lambda g: (g, gblk + i))

    def mspec(a):
        def imap(g):
            run = g * n_half + a
            return run // nt, jnp.where(run % nt < n_ct, 0, 1), 0, 0
        return pl.BlockSpec((None, None, 6, d), imap)

    flat = lambda arr: arr.reshape(n_tok, arr.shape[-1])
    xo, h2, route, counts = pl.pallas_call(
        functools.partial(_merge_kernel, tm=tm, n_half=n_half),
        grid=(n_tok // tile,),
        in_specs=[tok, tok, tok, gate(0), gate(1), gate(2), tok] + [mspec(a) for a in range(n_half)]
        + [pl.BlockSpec((1, d), lambda g: (0, 0)), wspec, wspec, wspec, wspec, rspec, rspec2,
           pl.BlockSpec((1, LANES), lambda g: (0, 0))],
        out_specs=[tok, tok, pl.BlockSpec((tile, LANES), lambda g: (g, 0)),
                   pl.BlockSpec((8, LANES), lambda g: (0, 0))],
        out_shape=[jax.ShapeDtypeStruct((n_tok, d), F32), jax.ShapeDtypeStruct((n_tok, d), F32),
                   jax.ShapeDtypeStruct((n_tok, LANES), F32), jax.ShapeDtypeStruct((8, LANES), F32)],
        scratch_shapes=[pltpu.VMEM((1, LANES), F32)],
        compiler_params=_cparams(("arbitrary",)),
        name="merge_route",
    )(flat(yg), flat(ya), flat(yr), flat(p), flat(p), flat(p), flat(xs), *([mod] * n_half),
      g2.reshape(1, d), wg, wa, wr, wo, wrh, wrl, br)
    return xo.reshape(b_, t_, d), h2.reshape(b_, t_, d), route.reshape(b_, t_, LANES), counts


ROW_DMA_UNROLL = 16


def _issue_rows(row_copy, tm):
    def issue(c, carry):
        for u in range(ROW_DMA_UNROLL):
            row_copy(c * ROW_DMA_UNROLL + u, 0).start()
            row_copy(c * ROW_DMA_UNROLL + u, 1).start()
        return carry

    lax.fori_loop(0, tm // ROW_DMA_UNROLL, issue, 0)


def _drain_rows(row_copy, tm):
    def drain(c, carry):
        for u in range(ROW_DMA_UNROLL):
            row_copy(c * ROW_DMA_UNROLL + u, 0).wait()
            row_copy(c * ROW_DMA_UNROLL + u, 1).wait()
        return carry

    lax.fori_loop(0, tm // ROW_DMA_UNROLL, drain, 0)


def _dispatch_kernel(plan_ref, dest_ref, h_ref, xs_ref, zbuf, sem, zsem, *, tm, n_blocks):
    def zero_block(row0):
        return pltpu.make_async_copy(zbuf, xs_ref.at[pl.ds(pl.multiple_of(row0, MOE_BLOCK), MOE_BLOCK)], zsem)

    @pl.when(jnp.logical_and(pl.program_id(0) == 0, pl.program_id(1) == 0))
    def _():
        zbuf[...] = jnp.zeros_like(zbuf)

        def region_tails(fn):
            for e in range(MOE_EXPERTS):
                @pl.when(plan_ref[MOE_EXPERTS + e] > 0)
                def _():
                    fn(zero_block(plan_ref[e] - MOE_BLOCK))

        def trailing(fn):
            def body(i, carry):
                fn(zero_block(i * MOE_BLOCK))
                return carry
            lax.fori_loop(plan_ref[2 * MOE_EXPERTS], n_blocks, body, 0)

        region_tails(lambda cp: cp.start())
        trailing(lambda cp: cp.start())
        region_tails(lambda cp: cp.wait())
        trailing(lambda cp: cp.wait())

    def row_copy(i, k):
        return pltpu.make_async_copy(h_ref.at[pl.ds(i, 1)], xs_ref.at[pl.ds(dest_ref[0, k * tm + i], 1)], sem)

    _issue_rows(row_copy, tm)
    _drain_rows(row_copy, tm)


def _dispatch(h2, dest, plan, n_slots):
    b_, t_, d = h2.shape
    tm = 256
    nt = t_ // tm
    grid_spec = pltpu.PrefetchScalarGridSpec(
        num_scalar_prefetch=1,
        grid=(b_, nt),
        in_specs=[pl.BlockSpec((None, 1, 2 * tm), lambda b, t, plan: (b * nt + t, 0, 0), memory_space=pltpu.SMEM),
                  pl.BlockSpec((tm, d), lambda b, t, plan: (b * nt + t, 0))],
        out_specs=pl.BlockSpec(memory_space=pl.ANY),
        scratch_shapes=[pltpu.VMEM((MOE_BLOCK, d), F32), pltpu.SemaphoreType.DMA, pltpu.SemaphoreType.DMA],
    )
    return pl.pallas_call(
        functools.partial(_dispatch_kernel, tm=tm, n_blocks=n_slots // MOE_BLOCK),
        grid_spec=grid_spec,
        out_shape=jax.ShapeDtypeStruct((n_slots, d), F32),
        compiler_params=_cparams(("arbitrary", "arbitrary")),
        name="moe_dispatch",
    )(plan, dest, h2.reshape(b_ * t_, d))


def _expert_kernel(be_ref, nu_ref, x_ref, w1_ref, w3_ref, w2_ref, y_ref, wb):
    i = pl.program_id(0)
    used = i < nu_ref[0]
    new_expert = jnp.logical_or(i == 0, be_ref[i] != be_ref[jnp.maximum(i - 1, 0)])

    @pl.when(jnp.logical_and(used, new_expert))
    def _():
        wb[0] = w1_ref[...].astype(BF16)
        wb[1] = w3_ref[...].astype(BF16)
        wb[2] = w2_ref[...].astype(BF16)

    @pl.when(used)
    def _():
        xb = x_ref[...].astype(BF16)
        h1 = _dot(xb, wb[0])
        h3 = _dot(xb, wb[1])
        y_ref[...] = _dot((_silu(h1) * h3).astype(BF16), wb[2])

    @pl.when(i >= nu_ref[0])
    def _():
        y_ref[...] = jnp.zeros_like(y_ref)


def _experts(x_sorted, blk_exp, n_used, w1, w3, w2, layer):
    n_slots, d = x_sorted.shape
    hid = w1.shape[-1]
    n_blocks = n_slots // MOE_BLOCK
    grid_spec = pltpu.PrefetchScalarGridSpec(
        num_scalar_prefetch=2,
        grid=(n_blocks,),
        in_specs=[pl.BlockSpec((MOE_BLOCK, d), lambda i, be, nu: (i, 0)),
                  pl.BlockSpec((None, None, d, hid), lambda i, be, nu: (layer, be[i], 0, 0)),
                  pl.BlockSpec((None, None, d, hid), lambda i, be, nu: (layer, be[i], 0, 0)),
                  pl.BlockSpec((None, None, hid, d), lambda i, be, nu: (layer, be[i], 0, 0))],
        out_specs=pl.BlockSpec((MOE_BLOCK, d), lambda i, be, nu: (i, 0)),
        scratch_shapes=[pltpu.VMEM((3, d, hid), BF16)],
    )
    assert d == hid
    return pl.pallas_call(
        _expert_kernel,
        grid_spec=grid_spec,
        out_shape=jax.ShapeDtypeStruct((n_slots, d), F32),
        compiler_params=_cparams(("arbitrary",)),
        name="moe_experts",
    )(blk_exp, n_used, x_sorted, w1, w3, w2)


def _combine_kernel(dcur_ref, dnext_ref, x_ref, route_ref, mod_ref, fg_ref, y_ref, o_ref, buf, sem, *,
                    tm, final, n_steps):
    g = pl.program_id(0)
    slot = g % 2

    def row_copy(dest_ref, sl):
        def make(i, k):
            return pltpu.make_async_copy(y_ref.at[pl.ds(dest_ref[0, k * tm + i], 1)],
                                         buf.at[sl, k, pl.ds(i, 1)], sem.at[sl])
        return make

    @pl.when(g == 0)
    def _():
        _issue_rows(row_copy(dcur_ref, 0), tm)

    @pl.when(g + 1 < n_steps)
    def _():
        _issue_rows(row_copy(dnext_ref, 1 - slot), tm)

    _drain_rows(row_copy(dcur_ref, slot), tm)

    r = route_ref[...]
    f = buf[slot, 0] * r[:, 2:3] + buf[slot, 1] * r[:, 3:4]
    x = x_ref[...] + mod_ref[5:6, :] * f
    if final:
        x = x * lax.rsqrt(jnp.mean(x * x, axis=-1, keepdims=True) + NORM_EPS) * fg_ref[...]
    o_ref[...] = x


def _combine(xs, route, mod, final_g, y_sorted, dest, n_ctx, final):
    b_, t_, d = xs.shape
    tm = 256
    nt = t_ // tm
    n_ct = n_ctx // tm
    skip = n_ct if final else 0
    ntl = nt - skip
    n_steps = b_ * ntl

    def bt(g):
        return g // ntl, g % ntl + skip

    def tok(w):
        return pl.BlockSpec((None, tm, w), lambda g: (*bt(g), 0))

    def dspec(off):
        def imap(g):
            b, t = bt(jnp.minimum(g + off, n_steps - 1))
            return b * nt + t, 0, 0
        return pl.BlockSpec((None, 1, 2 * tm), imap, memory_space=pltpu.SMEM)

    return pl.pallas_call(
        functools.partial(_combine_kernel, tm=tm, final=final, n_steps=n_steps),
        grid=(n_steps,),
        in_specs=[dspec(0), dspec(1), tok(d), tok(LANES),
                  pl.BlockSpec((None, None, 6, d), lambda g: (g // ntl, jnp.where(g % ntl + skip < n_ct, 0, 1), 0, 0)),
                  pl.BlockSpec((1, d), lambda g: (0, 0)),
                  pl.BlockSpec(memory_space=pl.ANY)],
        out_specs=pl.BlockSpec((None, tm, d), lambda g: (g // ntl, g % ntl, 0)),
        out_shape=jax.ShapeDtypeStruct((b_, t_ - skip * tm, d), F32),
        scratch_shapes=[pltpu.VMEM((2, 2, tm, d), F32), pltpu.SemaphoreType.DMA((2,))],
        compiler_params=_cparams(("arbitrary",)),
        name="moe_combine",
    )(dest, dest, xs, route, mod, final_g.reshape(1, d), y_sorted)


def _slot_plan(route, counts, tm):
    b_, t_, _ = route.shape
    n_tok = b_ * t_
    cnt = counts[0, EXPERT_LANE0:EXPERT_LANE0 + MOE_EXPERTS].astype(jnp.int32)
    padded = (cnt + MOE_BLOCK - 1) // MOE_BLOCK * MOE_BLOCK
    pad_end = jnp.cumsum(padded)
    pad_start = pad_end - padded
    n_blocks = (n_tok * 2 + MOE_EXPERTS * (MOE_BLOCK - 1) + MOE_BLOCK - 1) // MOE_BLOCK
    n_used = (pad_end[-1] // MOE_BLOCK).astype(jnp.int32)
    blk = jnp.minimum(jnp.arange(n_blocks, dtype=jnp.int32), n_used - 1) * MOE_BLOCK
    blk_exp = jnp.sum((pad_end[None, :] <= blk[:, None]).astype(jnp.int32), axis=1)
    blk_exp = jnp.minimum(blk_exp, MOE_EXPERTS - 1).astype(jnp.int32)
    e = route[..., 0:2].astype(jnp.int32)
    rank = route[..., 4:6].astype(jnp.int32)
    eid = jnp.arange(MOE_EXPERTS, dtype=jnp.int32)
    dest = rank + jnp.sum(jnp.where(e[..., None] == eid, pad_start, 0), axis=-1)
    dest = dest.reshape(b_ * t_ // tm, tm, 2).transpose(0, 2, 1).reshape(b_ * t_ // tm, 1, 2 * tm)
    plan = jnp.concatenate([pad_end, padded, n_used.reshape(1)]).astype(jnp.int32)
    return dest, blk_exp, n_used.reshape(1), plan, n_blocks * MOE_BLOCK


def _permute_w_in(w_in):
    parts = [w_in[:, _REF_OFF[name][0]:_REF_OFF[name][0] + _REF_OFF[name][1]] for name in _OUR_ORDER]
    width = sum(part.shape[1] for part in parts)
    parts.append(jnp.zeros((w_in.shape[0], N_P - width), w_in.dtype))
    return jnp.concatenate(parts, axis=1).astype(BF16)


def _rope_tables(n_ctx, l_):
    half = SCAN_DK // 2
    inv = ROPE_BASE ** (-jnp.arange(half, dtype=F32) / half)
    ang = jnp.arange(n_ctx + l_, dtype=F32)[:, None] * inv[None, :]
    ret_cos = jnp.concatenate([jnp.cos(ang), jnp.cos(ang)], axis=-1)
    ret_sin = jnp.concatenate([-jnp.sin(ang), jnp.sin(ang)], axis=-1)
    q = ATT_HEAD_DIM // 4
    inv = ROPE_BASE ** (-jnp.arange(q, dtype=F32) / q)
    tpos = jnp.arange(l_)
    a_r = (tpos // GRID_W).astype(F32)[:, None] * inv[None, :]
    a_c = (tpos % GRID_W).astype(F32)[:, None] * inv[None, :]
    att_cos = jnp.concatenate([jnp.cos(a_r), jnp.cos(a_r), jnp.cos(a_c), jnp.cos(a_c)], axis=-1)
    att_sin = jnp.concatenate([-jnp.sin(a_r), jnp.sin(a_r), -jnp.sin(a_c), jnp.sin(a_c)], axis=-1)
    q_scale = ATT_HEAD_DIM ** -0.5
    return ret_cos, ret_sin, att_cos, att_sin, att_cos * q_scale, att_sin * q_scale


def kernel(x, c, ctx, c_ctx, w_ada, b_ada, norm1_g, norm2_g, w_in, gla_wa2, gla_ba, gla_norm_g, attn_sink,
           ret_norm_g, w_br_gla, w_br_attn, w_br_ret, w_out, moe_w_grp, moe_b_grp, moe_w_exp, moe_b_exp,
           moe_w1, moe_w3, moe_w2, final_g):
    b_, l_, d = x.shape
    n_ctx = ctx.shape[1]
    depth = w_ada.shape[0]
    assert d == D_MODEL and n_ctx % 256 == 0 and l_ % 256 == 0 and b_ <= 8

    xs = jnp.concatenate([ctx, x], axis=1)
    cc = jnp.zeros((16, d), F32).at[:b_].set(c).at[b_].set(c_ctx)
    mod_all = _modulation(cc, w_ada, b_ada)
    ret_cos, ret_sin, att_cos, att_sin, att_cos_q, att_sin_q = _rope_tables(n_ctx, l_)

    for layer in range(depth):
        last = layer == depth - 1
        m = mod_all[layer].reshape(16, 6, d)
        mod = jnp.stack([jnp.broadcast_to(m[b_], (b_, 6, d)), m[:b_]], axis=1)

        w_p = _permute_w_in(w_in[layer])
        wa2_pad = jnp.zeros((2, LANES, 512), F32)
        wa2_pad = wa2_pad.at[0, 0:GLA_LOW_RANK].set(gla_wa2[layer, 0])
        wa2_pad = wa2_pad.at[1, GLA_LOW_RANK:2 * GLA_LOW_RANK].set(gla_wa2[layer, 1]).astype(BF16)
        ba = gla_ba[layer].reshape(2, 1, 512)

        p = _in_projection(xs, mod, norm1_g[layer], w_p, n_ctx)
        obg, obr = _scan(p, ret_cos, ret_sin, wa2_pad, ba, n_ctx, True)
        yg, yr = _scan(p, ret_cos, ret_sin, wa2_pad, ba, n_ctx, False,
                       (obg, obr, gla_norm_g[layer], ret_norm_g[layer]))
        ya = _attention(p, attn_sink[layer], att_cos, att_sin, att_cos_q, att_sin_q, n_ctx)

        w_route = jnp.zeros((d, LANES), F32)
        w_route = w_route.at[:, 0:MOE_GROUPS].set(moe_w_grp[layer])
        w_route = w_route.at[:, EXPERT_LANE0:EXPERT_LANE0 + MOE_EXPERTS].set(moe_w_exp[layer])
        wr_hi = w_route.astype(BF16)
        wr_lo = (w_route - wr_hi.astype(F32)).astype(BF16)
        b_route = jnp.zeros((1, LANES), F32)
        b_route = b_route.at[0, 0:MOE_GROUPS].set(moe_b_grp[layer])
        b_route = b_route.at[0, EXPERT_LANE0:EXPERT_LANE0 + MOE_EXPERTS].set(moe_b_exp[layer])

        xs, h2, route, counts = _merge(
            yg, ya, yr, p, xs, mod, norm2_g[layer],
            w_br_gla[layer].astype(BF16), w_br_attn[layer].astype(BF16), w_br_ret[layer].astype(BF16),
            w_out[layer].astype(BF16), wr_hi, jnp.concatenate([wr_hi, wr_lo], axis=1), b_route, n_ctx)

        dest, blk_exp, n_used, plan, n_slots = _slot_plan(route, counts, 256)
        x_sorted = _dispatch(h2, dest, plan, n_slots)
        y_sorted = _experts(x_sorted, blk_exp, n_used, moe_w1, moe_w3, moe_w2, layer)
        xs = _combine(xs, route, mod, final_g, y_sorted, dest, n_ctx, last)

    return xs
```

```python
import functools
import math

import jax
import jax.numpy as jnp
from jax import lax
from jax.experimental import pallas as pl
from jax.experimental.pallas import tpu as pltpu

F32 = jnp.float32
BF16 = jnp.bfloat16

D_MODEL = 1024
GRID_W = 64
CHUNK = 128
NORM_EPS = 1e-6
ROPE_BASE = 10000.0

SCAN_HEADS = 4
SCAN_DK = 128
SCAN_DV = 256
GLA_LOW_RANK = 16
GLA_TAU = 16.0

ATT_HEAD_DIM = 128
ATT_Q_HEADS = 8
ATT_KV_HEADS = 2
ATT_GROUP = ATT_Q_HEADS // ATT_KV_HEADS

MOE_GROUPS = 4
MOE_EPG = 8
MOE_EXPERTS = 32
MOE_BLOCK = 512
EXPERT_LANE0 = 32

LANES = 128
INPROJ_TN = 2816
NEG = -1e30

_REF_LAYOUT = (
    ('gla_q', 512), ('gla_k', 512), ('gla_v', 1024), ('gla_r', 1024), ('gla_lr', 32),
    ('att_q', 1024), ('att_k', 256), ('att_v', 256),
    ('ret_q', 512), ('ret_k', 512), ('ret_v', 1024), ('ret_g', 1024), ('gates', 3072),
)
_OUR_ORDER = ('gla_v', 'gla_r', 'att_q', 'ret_v', 'ret_g', 'gates', 'gla_q', 'gla_k', 'ret_q', 'ret_k',
              'att_k', 'att_v', 'gla_lr')


def _layout():
    ref_off, start = {}, 0
    for name, width in _REF_LAYOUT:
        ref_off[name] = (start, width)
        start += width
    col, off = {}, 0
    for name in _OUR_ORDER:
        col[name] = off
        off += ref_off[name][1]
    n_p = (off + INPROJ_TN - 1) // INPROJ_TN * INPROJ_TN
    return ref_off, col, n_p


_REF_OFF, COL, N_P = _layout()
VMEM_LIMIT = 56 * 1024 * 1024


def _cparams(sem):
    return pltpu.CompilerParams(dimension_semantics=sem, vmem_limit_bytes=VMEM_LIMIT)


def _silu(x):
    return x / (1.0 + jnp.exp(-x))


def _sigmoid(x):
    return 1.0 / (1.0 + jnp.exp(-x))


def _dot(a, b):
    return jnp.dot(a, b, preferred_element_type=F32)


def _dot_nt(a, b):
    return lax.dot_general(a, b, (((1,), (1,)), ((), ())), preferred_element_type=F32)


def _dot_tn(a, b):
    return lax.dot_general(a, b, (((0,), (0,)), ((), ())), preferred_element_type=F32)


def _mod_kernel(c_ref, w_ref, b_ref, o_ref):
    s = _silu(c_ref[...]).astype(BF16)
    o_ref[...] = _dot(s, w_ref[...].astype(BF16)) + b_ref[...]


def _modulation(cc, w_ada, b_ada):
    depth, d, n6 = w_ada.shape
    rows = cc.shape[0]
    tn = 512
    return pl.pallas_call(
        _mod_kernel,
        grid=(depth, n6 // tn),
        in_specs=[pl.BlockSpec((rows, d), lambda l, j: (0, 0)),
                  pl.BlockSpec((None, d, tn), lambda l, j: (l, 0, j)),
                  pl.BlockSpec((None, 1, tn), lambda l, j: (l, 0, j))],
        out_specs=pl.BlockSpec((None, rows, tn), lambda l, j: (l, 0, j)),
        out_shape=jax.ShapeDtypeStruct((depth, rows, n6), F32),
        compiler_params=_cparams(("arbitrary", "arbitrary")),
        name="modulation",
    )(cc, w_ada, b_ada.reshape(depth, 1, n6))


def _inproj_kernel(x_ref, mod_ref, g_ref, w_ref, o_ref, h_scr, *, n_ctx, tm):
    t = pl.program_id(1)

    @pl.when(pl.program_id(2) == 0)
    def _():
        x = x_ref[...]
        y = x * lax.rsqrt(jnp.mean(x * x, axis=-1, keepdims=True) + NORM_EPS) * g_ref[...]
        row = t * tm + lax.broadcasted_iota(jnp.int32, (tm, 1), 0)
        is_ctx = row < n_ctx
        shift = jnp.where(is_ctx, mod_ref[0, 0:1, :], mod_ref[1, 0:1, :])
        scale = jnp.where(is_ctx, mod_ref[0, 1:2, :], mod_ref[1, 1:2, :])
        h_scr[...] = (y * (1.0 + scale) + shift).astype(BF16)

    o_ref[...] = _dot(h_scr[...], w_ref[...]).astype(o_ref.dtype)


def _in_projection(xs, mod, g, w_p, n_ctx):
    b_, t_, d = xs.shape
    tm = 1408 if t_ % 1408 == 0 else (768 if t_ % 768 == 0 else 256)
    tn = INPROJ_TN
    return pl.pallas_call(
        functools.partial(_inproj_kernel, n_ctx=n_ctx, tm=tm),
        grid=(b_, t_ // tm, N_P // tn),
        in_specs=[pl.BlockSpec((None, tm, d), lambda b, t, j: (b, t, 0)),
                  pl.BlockSpec((None, 2, 6, d), lambda b, t, j: (b, 0, 0, 0)),
                  pl.BlockSpec((1, d), lambda b, t, j: (0, 0)),
                  pl.BlockSpec((d, tn), lambda b, t, j: (0, j))],
        out_specs=pl.BlockSpec((None, tm, tn), lambda b, t, j: (b, t, j)),
        out_shape=jax.ShapeDtypeStruct((b_, t_, N_P), BF16),
        scratch_shapes=[pltpu.VMEM((tm, d), BF16)],
        compiler_params=_cparams(("arbitrary", "arbitrary", "arbitrary")),
        name="in_projection",
    )(xs, mod, g.reshape(1, d), w_p)


def _chunk_update(q_in, k_in, k_out, v_bf, dec, s_ref, idx, mask):
    qb = q_in.astype(BF16)
    sc = jnp.where(mask, _dot_nt(qb, k_in.astype(BF16)), 0.0)
    st = s_ref[idx]
    o = _dot(sc.astype(BF16), v_bf) + _dot_nt(qb, st.astype(BF16))
    s_ref[idx] = st * dec + _dot_tn(v_bf, k_out.astype(BF16))
    return o


def _split3(x):
    hi = x.astype(BF16)
    r = x - hi.astype(F32)
    mid = r.astype(BF16)
    lo = (r - mid.astype(F32)).astype(BF16)
    return hi, mid, lo


def _group_norm_gate(o, g, r):
    mu = jnp.mean(o, axis=-1, keepdims=True)
    oc = o - mu
    var = jnp.mean(oc * oc, axis=-1, keepdims=True)
    return oc * lax.rsqrt(var + NORM_EPS) * g * _silu(r)


SCAN_BATCH_BLOCK = 4
SCAN_BATCH_BLOCK_BWD = 8


def _scan_kernel(*refs, backward, bb_n):
    if backward:
        (gq, gk, gv, lr, rq, rk, rv, cos, sin, wa2, ba, obg, obr, sg, sr) = refs
    else:
        (gq, gk, gv, lr, rq, rk, rv, cos, sin, wa2, ba, gr, rg, obg, obr, gng, rng, yg, yr, sg, sr) = refs
    c_ = CHUNK

    @pl.when(pl.program_id(1) == 0)
    def _():
        sg[...] = jnp.zeros_like(sg)
        sr[...] = jnp.zeros_like(sr)

    row = lax.broadcasted_iota(jnp.int32, (c_, c_), 0)
    col = lax.broadcasted_iota(jnp.int32, (c_, c_), 1)
    if backward:
        tri = jnp.where(col >= row, 1.0, 0.0).astype(BF16)
        mask = col > row
    else:
        tri = jnp.where(col <= row, 1.0, 0.0).astype(BF16)
        mask = col <= row
    last = 0 if backward else c_ - 1

    pos = lax.broadcasted_iota(jnp.int32, (c_, SCAN_DK), 0)
    steps = ((c_ - pos) if backward else (pos + 1)).astype(F32)
    cosv = cos[...]
    sinv = sin[...]
    scale = SCAN_DK ** -0.5

    for bb in range(bb_n):
        for h in range(SCAN_HEADS):
            ks = slice(h * SCAN_DK, (h + 1) * SCAN_DK)
            vs = slice(h * SCAN_DV, (h + 1) * SCAN_DV)
            ld = math.log(1.0 - 2.0 ** (-5.0 - h))
            c = steps * ld
            q = rq[bb, :, ks].astype(F32)
            k = rk[bb, :, ks].astype(F32)
            q = q * cosv + pltpu.roll(q, SCAN_DK // 2, 1) * sinv
            k = (k * cosv + pltpu.roll(k, SCAN_DK // 2, 1) * sinv) * scale
            dec = jnp.full((1, SCAN_DK), math.exp(c_ * ld), F32)
            k_in = k * jnp.exp(-c)
            o = _chunk_update(q * jnp.exp(c), k_in, k_in * math.exp(c_ * ld), rv[bb, :, vs], dec, sr, (bb, h), mask)
            if backward:
                obr[bb, :, vs] = o
            else:
                yr[bb, :, vs] = _group_norm_gate(o + obr[bb, :, vs], rng[:, vs],
                                                 rg[bb, :, vs].astype(F32)).astype(yr.dtype)

    for bb in range(bb_n):
        pre = _dot(lr[bb], wa2[...]) + ba[...]
        la = (jnp.minimum(pre, 0.0) - jnp.log1p(jnp.exp(-jnp.abs(pre)))) * (1.0 / GLA_TAU)
        hi, mid, lo = _split3(la)
        cum = _dot(tri, hi) + _dot(tri, mid) + _dot(tri, lo)
        tot = cum[last:last + 1, :]
        for h in range(SCAN_HEADS):
            ks = slice(h * SCAN_DK, (h + 1) * SCAN_DK)
            vs = slice(h * SCAN_DV, (h + 1) * SCAN_DV)
            c = cum[:, ks]
            th = tot[:, ks]
            q = gq[bb, :, ks].astype(F32) * scale
            k = gk[bb, :, ks].astype(F32)
            dec = jnp.exp(th)
            k_in = k * jnp.exp(-c)
            o = _chunk_update(q * jnp.exp(c), k_in, k_in * dec, gv[bb, :, vs], dec, sg, (bb, h), mask)
            if backward:
                obg[bb, :, vs] = o
            else:
                yg[bb, :, vs] = _group_norm_gate(o + obg[bb, :, vs], gng[:, vs],
                                                 gr[bb, :, vs].astype(F32)).astype(yg.dtype)


def _scan(p, ret_cos, ret_sin, wa2_pad, ba, n_ctx, backward, extras=None):
    b_, t_, _ = p.shape
    n_chunks = t_ // CHUNK
    n_cc = n_ctx // CHUNK
    d = D_MODEL
    want = SCAN_BATCH_BLOCK_BWD if backward else SCAN_BATCH_BLOCK
    bb_n = want if b_ % want == 0 else 1

    if backward:
        def cidx(s):
            return jnp.where(s < n_cc, n_cc - 1 - s, n_chunks - 1 + n_cc - s)
    else:
        def cidx(s):
            return s

    def pspec(name, width):
        blk = COL[name] // width
        return pl.BlockSpec((bb_n, CHUNK, width), lambda b, s: (b, cidx(s), blk))

    dirn = 1 if backward else 0
    tok = lambda width: pl.BlockSpec((bb_n, CHUNK, width), lambda b, s: (b, cidx(s), 0))
    in_specs = [pspec('gla_q', 512), pspec('gla_k', 512), pspec('gla_v', 1024), pspec('gla_lr', LANES),
                pspec('ret_q', 512), pspec('ret_k', 512), pspec('ret_v', 1024),
                pl.BlockSpec((CHUNK, SCAN_DK), lambda b, s: (cidx(s), 0)),
                pl.BlockSpec((CHUNK, SCAN_DK), lambda b, s: (cidx(s), 0)),
                pl.BlockSpec((None, LANES, 512), lambda b, s: (dirn, 0, 0)),
                pl.BlockSpec((None, 1, 512), lambda b, s: (dirn, 0, 0))]
    args = [p, p, p, p, p, p, p, ret_cos, ret_sin, wa2_pad, ba]
    state = [pltpu.VMEM((bb_n, SCAN_HEADS, SCAN_DV, SCAN_DK), F32),
             pltpu.VMEM((bb_n, SCAN_HEADS, SCAN_DV, SCAN_DK), F32)]
    if backward:
        out_specs = [tok(d), tok(d)]
        out_shape = [jax.ShapeDtypeStruct((b_, t_, d), F32)] * 2
    else:
        obg, obr, gng, rng = extras
        in_specs += [pspec('gla_r', 1024), pspec('ret_g', 1024), tok(d), tok(d),
                     pl.BlockSpec((1, d), lambda b, s: (0, 0)), pl.BlockSpec((1, d), lambda b, s: (0, 0))]
        args += [p, p, obg, obr, gng.reshape(1, d), rng.reshape(1, d)]
        out_specs = [tok(d), tok(d)]
        out_shape = [jax.ShapeDtypeStruct((b_, t_, d), BF16)] * 2
    return pl.pallas_call(
        functools.partial(_scan_kernel, backward=backward, bb_n=bb_n),
        grid=(b_ // bb_n, n_chunks),
        in_specs=in_specs, out_specs=out_specs, out_shape=out_shape,
        scratch_shapes=state,
        compiler_params=_cparams(("arbitrary", "arbitrary")),
        name="scan_bwd" if backward else "scan_fwd",
    )(*args)


def _rope_axial(x, cosv, sinv, lane_lo):
    partner = jnp.where(lane_lo, pltpu.roll(x, 96, 1), pltpu.roll(x, 32, 1))
    return x * cosv + partner * sinv


def _attn_block(refs, bb, local, first, last):
    (sink, q, kp, ks, kn, vp, vs, vn, kc, vc, cp, cs, cn, sp, ss, sn, cq, sq, o_ref) = refs
    qb_ = q.shape[1]
    hd = ATT_HEAD_DIM
    scale = hd ** -0.5
    rows = ATT_GROUP * qb_
    rowi = lax.broadcasted_iota(jnp.int32, (rows, 1), 0)
    if local:
        lane = lax.broadcasted_iota(jnp.int32, (qb_, hd), 1)
        lane_lo = (lane & 63) < 32
        i_idx = lax.broadcasted_iota(jnp.int32, (rows, qb_), 0) & (qb_ - 1)
        j_idx = lax.broadcasted_iota(jnp.int32, (rows, qb_), 1)
        mask_p = jnp.logical_and(j_idx >= i_idx, jnp.logical_not(first))
        mask_n = jnp.logical_and(j_idx <= i_idx, jnp.logical_not(last))

    for g in range(ATT_KV_HEADS):
        gs = slice(g * hd, (g + 1) * hd)
        heads = []
        for hh in range(ATT_GROUP):
            qs = slice((g * ATT_GROUP + hh) * hd, (g * ATT_GROUP + hh + 1) * hd)
            qh = q[bb, :, qs].astype(F32)
            qh = _rope_axial(qh, cq[...], sq[...], lane_lo) if local else qh * scale
            heads.append(qh.astype(BF16))
        q4 = jnp.concatenate(heads, axis=0)
        sink_col = jnp.full((rows, 1), sink[g * ATT_GROUP], F32)
        for hh in range(1, ATT_GROUP):
            sink_col = jnp.where(rowi >= hh * qb_, sink[g * ATT_GROUP + hh], sink_col)

        s_c = _dot_nt(q4, kc[bb, :, gs])
        n_c = s_c.shape[1] // qb_
        m_el = s_c[:, 0:qb_]
        for cpart in range(1, n_c):
            m_el = jnp.maximum(m_el, s_c[:, cpart * qb_:(cpart + 1) * qb_])
        if local:
            k_p = _rope_axial(kp[bb, :, gs].astype(F32), cp[...], sp[...], lane_lo).astype(BF16)
            k_s = _rope_axial(ks[bb, :, gs].astype(F32), cs[...], ss[...], lane_lo).astype(BF16)
            k_n = _rope_axial(kn[bb, :, gs].astype(F32), cn[...], sn[...], lane_lo).astype(BF16)
            s_p = jnp.where(mask_p, _dot_nt(q4, k_p), NEG)
            s_s = _dot_nt(q4, k_s)
            s_n = jnp.where(mask_n, _dot_nt(q4, k_n), NEG)
            m_el = jnp.maximum(jnp.maximum(m_el, s_p), jnp.maximum(s_s, s_n))
        m = jnp.maximum(sink_col, jnp.max(m_el, axis=-1, keepdims=True))
        p_c = jnp.exp(s_c - m)
        d_el = p_c[:, 0:qb_]
        for cpart in range(1, n_c):
            d_el = d_el + p_c[:, cpart * qb_:(cpart + 1) * qb_]
        acc = _dot(p_c.astype(BF16), vc[bb, :, gs])
        if local:
            p_p = jnp.exp(s_p - m)
            p_s = jnp.exp(s_s - m)
            p_n = jnp.exp(s_n - m)
            d_el = d_el + p_p + p_s + p_n
            acc = acc + _dot(p_p.astype(BF16), vp[bb, :, gs]) + _dot(p_s.astype(BF16), vs[bb, :, gs]) \
                + _dot(p_n.astype(BF16), vn[bb, :, gs])
        den = jnp.exp(sink_col - m) + jnp.sum(d_el, axis=-1, keepdims=True)
        out = acc / den
        for hh in range(ATT_GROUP):
            qs = slice((g * ATT_GROUP + hh) * hd, (g * ATT_GROUP + hh + 1) * hd)
            o_ref[bb, :, qs] = out[hh * qb_:(hh + 1) * qb_, :].astype(o_ref.dtype)


ATTN_BATCH_BLOCK = 4


def _attn_kernel(*refs, n_cc, nb, bb_n):
    n = pl.program_id(1)

    @pl.when(n < n_cc)
    def _():
        for bb in range(bb_n):
            _attn_block(refs, bb, False, None, None)

    @pl.when(n >= n_cc)
    def _():
        for bb in range(bb_n):
            _attn_block(refs, bb, True, n == n_cc, n == n_cc + nb - 1)


def _attention(p, sink, att_cos, att_sin, att_cos_q, att_sin_q, n_ctx):
    b_, t_, _ = p.shape
    qb_ = 128
    n_cc = n_ctx // qb_
    nb = (t_ - n_ctx) // qb_
    d = D_MODEL
    bb_n = ATTN_BATCH_BLOCK if b_ % ATTN_BATCH_BLOCK == 0 else 1
    qblk, kblk, vblk = COL['att_q'] // 1024, COL['att_k'] // 256, COL['att_v'] // 256
    smem = pl.BlockSpec(memory_space=pltpu.SMEM)

    prev = lambda n: jnp.maximum(n - 1, n_cc)
    this = lambda n: jnp.maximum(n, n_cc)
    nxt = lambda n: jnp.maximum(jnp.minimum(n + 1, n_cc + nb - 1), n_cc)

    def tspec(width, blk, fn):
        return pl.BlockSpec((bb_n, qb_, width), lambda b, n: (b, fn(n), blk))

    def tab(fn):
        return pl.BlockSpec((qb_, ATT_HEAD_DIM), lambda b, n: (fn(n) - n_cc, 0))

    ctx_k = pl.BlockSpec((bb_n, n_ctx, 256), lambda b, n: (b, 0, kblk))
    ctx_v = pl.BlockSpec((bb_n, n_ctx, 256), lambda b, n: (b, 0, vblk))
    return pl.pallas_call(
        functools.partial(_attn_kernel, n_cc=n_cc, nb=nb, bb_n=bb_n),
        grid=(b_ // bb_n, n_cc + nb),
        in_specs=[smem, tspec(1024, qblk, lambda n: n),
                  tspec(256, kblk, prev), tspec(256, kblk, this), tspec(256, kblk, nxt),
                  tspec(256, vblk, prev), tspec(256, vblk, this), tspec(256, vblk, nxt),
                  ctx_k, ctx_v, tab(prev), tab(this), tab(nxt), tab(prev), tab(this), tab(nxt), tab(this), tab(this)],
        out_specs=pl.BlockSpec((bb_n, qb_, d), lambda b, n: (b, n, 0)),
        out_shape=jax.ShapeDtypeStruct((b_, t_, d), BF16),
        compiler_params=_cparams(("arbitrary", "arbitrary")),
        name="attention",
    )(sink, p, p, p, p, p, p, p, p, p, att_cos, att_cos, att_cos, att_sin, att_sin, att_sin, att_cos_q, att_sin_q)


def _merge_kernel(*refs, tm, n_half):
    (yg, ya, yr, pg, pa, pr, x_ref), mods = refs[:7], refs[7:7 + n_half]
    (g2_ref, wg, wa, wr, wo, wrh, wrl, br, xo_ref, h2_ref, route_ref, cnt_ref, cnt_scr) = refs[7 + n_half:]
    tile = tm * n_half

    @pl.when(pl.program_id(0) == 0)
    def _():
        cnt_scr[...] = jnp.zeros_like(cnt_scr)

    def mod_row(i):
        out = mods[0][i:i + 1, :]
        if n_half > 1:
            run = lax.broadcasted_iota(jnp.int32, (tile, 1), 0) // tm
            for a in range(1, n_half):
                out = jnp.where(run == a, mods[a][i:i + 1, :], out)
        return out

    merged = (_sigmoid(pg[...].astype(F32)) * _dot(yg[...], wg[...])
              + _sigmoid(pa[...].astype(F32)) * _dot(ya[...], wa[...])
              + _sigmoid(pr[...].astype(F32)) * _dot(yr[...], wr[...]))
    mix = _dot(merged.astype(BF16), wo[...])
    x = x_ref[...] + mod_row(2) * mix
    xo_ref[...] = x
    y = x * lax.rsqrt(jnp.mean(x * x, axis=-1, keepdims=True) + NORM_EPS) * g2_ref[...]
    h2 = y * (1.0 + mod_row(4)) + mod_row(3)
    h2_ref[...] = h2

    h_hi = h2.astype(BF16)
    h_lo = (h2 - h_hi.astype(F32)).astype(BF16)
    both_w = _dot(h_hi, wrl[...])
    logits = both_w[:, :LANES] + both_w[:, LANES:] + _dot(h_lo, wrh[...]) + br[...]

    lane = lax.broadcasted_iota(jnp.int32, (tile, LANES), 1)
    lanef = lane.astype(F32)
    is_grp = lane < MOE_GROUPS
    gl = jnp.where(is_grp, logits, NEG)
    gmax = jnp.max(gl, axis=-1, keepdims=True)
    gidx = jnp.min(jnp.where(gl == gmax, lanef, float(LANES)), axis=-1, keepdims=True)
    gprob = 1.0 / jnp.sum(jnp.where(is_grp, jnp.exp(gl - gmax), 0.0), axis=-1, keepdims=True)
    lo = EXPERT_LANE0 + MOE_EPG * gidx
    el = jnp.where(jnp.logical_and(lanef >= lo, lanef < lo + MOE_EPG), logits, NEG)
    t1 = jnp.max(el, axis=-1, keepdims=True)
    i1 = jnp.min(jnp.where(el == t1, lanef, float(LANES)), axis=-1, keepdims=True)
    el2 = jnp.where(lanef == i1, NEG, el)
    t2 = jnp.max(el2, axis=-1, keepdims=True)
    i2 = jnp.min(jnp.where(el2 == t2, lanef, float(LANES)), axis=-1, keepdims=True)
    ex = jnp.exp(t2 - t1)
    w0 = gprob / (1.0 + ex)
    w1 = gprob * ex / (1.0 + ex)

    oh0 = lanef == i1
    oh1 = lanef == i2
    both = jnp.where(jnp.logical_or(oh0, oh1), 1.0, 0.0)
    rr = lax.broadcasted_iota(jnp.int32, (tile, tile), 0)
    cc = lax.broadcasted_iota(jnp.int32, (tile, tile), 1)
    tri = jnp.where(cc < rr, 1.0, 0.0).astype(BF16)
    base = cnt_scr[...] + _dot(tri, both.astype(BF16))
    r0 = jnp.sum(jnp.where(oh0, base, 0.0), axis=-1, keepdims=True)
    r1 = jnp.sum(jnp.where(oh1, base, 0.0), axis=-1, keepdims=True)
    cnt = cnt_scr[...] + jnp.sum(both, axis=0, keepdims=True)
    cnt_scr[...] = cnt
    cnt_ref[...] = jnp.broadcast_to(cnt, cnt_ref.shape)

    out = jnp.where(lane == 0, i1 - EXPERT_LANE0, 0.0)
    out = jnp.where(lane == 1, i2 - EXPERT_LANE0, out)
    out = jnp.where(lane == 2, w0, out)
    out = jnp.where(lane == 3, w1, out)
    out = jnp.where(lane == 4, r0, out)
    out = jnp.where(lane == 5, r1, out)
    route_ref[...] = out


def _merge(yg, ya, yr, p, xs, mod, g2, wg, wa, wr, wo, wrh, wrl, br, n_ctx):
    b_, t_, d = xs.shape
    n_tok = b_ * t_
    tm = 256
    nt = t_ // tm
    n_ct = n_ctx // tm
    n_half = 2 if n_tok % (2 * tm) == 0 else 1
    tile = n_half * tm
    gblk = COL['gates'] // d
    tok = pl.BlockSpec((tile, d), lambda g: (g, 0))
    wspec = pl.BlockSpec((d, d), lambda g: (0, 0))
    rspec = pl.BlockSpec((d, LANES), lambda g: (0, 0))
    rspec2 = pl.BlockSpec((d, 2 * LANES), lambda g: (0, 0))

    def gate(i):
        return pl.BlockSpec((tile, d), lambda g: (g, gblk + i))

    def mspec(a):
        def imap(g):
            run = g * n_half + a
            return run // nt, jnp.where(run % nt < n_ct, 0, 1), 0, 0
        return pl.BlockSpec((None, None, 6, d), imap)

    flat = lambda arr: arr.reshape(n_tok, arr.shape[-1])
    xo, h2, route, counts = pl.pallas_call(
        functools.partial(_merge_kernel, tm=tm, n_half=n_half),
        grid=(n_tok // tile,),
        in_specs=[tok, tok, tok, gate(0), gate(1), gate(2), tok] + [mspec(a) for a in range(n_half)]
        + [pl.BlockSpec((1, d), lambda g: (0, 0)), wspec, wspec, wspec, wspec, rspec, rspec2,
           pl.BlockSpec((1, LANES), lambda g: (0, 0))],
        out_specs=[tok, tok, pl.BlockSpec((tile, LANES), lambda g: (g, 0)),
                   pl.BlockSpec((8, LANES), lambda g: (0, 0))],
        out_shape=[jax.ShapeDtypeStruct((n_tok, d), F32), jax.ShapeDtypeStruct((n_tok, d), F32),
                   jax.ShapeDtypeStruct((n_tok, LANES), F32), jax.ShapeDtypeStruct((8, LANES), F32)],
        scratch_shapes=[pltpu.VMEM((1, LANES), F32)],
        compiler_params=_cparams(("arbitrary",)),
        name="merge_route",
    )(flat(yg), flat(ya), flat(yr), flat(p), flat(p), flat(p), flat(xs), *([mod] * n_half),
      g2.reshape(1, d), wg, wa, wr, wo, wrh, wrl, br)
    return xo.reshape(b_, t_, d), h2.reshape(b_, t_, d), route.reshape(b_, t_, LANES), counts


ROW_DMA_UNROLL = 16


def _issue_rows(row_copy, tm):
    def issue(c, carry):
        for u in range(ROW_DMA_UNROLL):
            row_copy(c * ROW_DMA_UNROLL + u, 0).start(priority=0)
            row_copy(c * ROW_DMA_UNROLL + u, 1).start(priority=1)
        return carry

    lax.fori_loop(0, tm // ROW_DMA_UNROLL, issue, 0)


def _drain_rows(row_copy, tm):
    def drain(c, carry):
        for u in range(ROW_DMA_UNROLL):
            row_copy(c * ROW_DMA_UNROLL + u, 0).wait()
            row_copy(c * ROW_DMA_UNROLL + u, 1).wait()
        return carry

    lax.fori_loop(0, tm // ROW_DMA_UNROLL, drain, 0)


def _dispatch_kernel(plan_ref, dest_ref, h_ref, xs_ref, zbuf, sem, zsem, *, tm, n_blocks):
    def zero_block(row0):
        return pltpu.make_async_copy(zbuf, xs_ref.at[pl.ds(pl.multiple_of(row0, MOE_BLOCK), MOE_BLOCK)], zsem)

    @pl.when(jnp.logical_and(pl.program_id(0) == 0, pl.program_id(1) == 0))
    def _():
        zbuf[...] = jnp.zeros_like(zbuf)

        def region_tails(fn):
            for e in range(MOE_EXPERTS):
                @pl.when(plan_ref[MOE_EXPERTS + e] > 0)
                def _():
                    fn(zero_block(plan_ref[e] - MOE_BLOCK))

        def trailing(fn):
            def body(i, carry):
                fn(zero_block(i * MOE_BLOCK))
                return carry
            lax.fori_loop(plan_ref[2 * MOE_EXPERTS], n_blocks, body, 0)

        region_tails(lambda cp: cp.start())
        trailing(lambda cp: cp.start())
        region_tails(lambda cp: cp.wait())
        trailing(lambda cp: cp.wait())

    def row_copy(i, k):
        return pltpu.make_async_copy(h_ref.at[pl.ds(i, 1)], xs_ref.at[pl.ds(dest_ref[0, k * tm + i], 1)], sem)

    _issue_rows(row_copy, tm)
    _drain_rows(row_copy, tm)


def _dispatch(h2, dest, plan, n_slots):
    b_, t_, d = h2.shape
    tm = 256
    nt = t_ // tm
    grid_spec = pltpu.PrefetchScalarGridSpec(
        num_scalar_prefetch=1,
        grid=(b_, nt),
        in_specs=[pl.BlockSpec((None, 1, 2 * tm), lambda b, t, plan: (b * nt + t, 0, 0), memory_space=pltpu.SMEM),
                  pl.BlockSpec((tm, d), lambda b, t, plan: (b * nt + t, 0))],
        out_specs=pl.BlockSpec(memory_space=pl.ANY),
        scratch_shapes=[pltpu.VMEM((MOE_BLOCK, d), F32), pltpu.SemaphoreType.DMA, pltpu.SemaphoreType.DMA],
    )
    return pl.pallas_call(
        functools.partial(_dispatch_kernel, tm=tm, n_blocks=n_slots // MOE_BLOCK),
        grid_spec=grid_spec,
        out_shape=jax.ShapeDtypeStruct((n_slots, d), F32),
        compiler_params=_cparams(("arbitrary", "arbitrary")),
        name="moe_dispatch",
    )(plan, dest, h2.reshape(b_ * t_, d))


def _expert_kernel(be_ref, nu_ref, x_ref, w1_ref, w3_ref, w2_ref, y_ref, wb):
    i = pl.program_id(0)
    used = i < nu_ref[0]
    new_expert = jnp.logical_or(i == 0, be_ref[i] != be_ref[jnp.maximum(i - 1, 0)])

    @pl.when(jnp.logical_and(used, new_expert))
    def _():
        wb[0] = w1_ref[...].astype(BF16)
        wb[1] = w3_ref[...].astype(BF16)
        wb[2] = w2_ref[...].astype(BF16)

    @pl.when(used)
    def _():
        xb = x_ref[...].astype(BF16)
        h1 = _dot(xb, wb[0])
        h3 = _dot(xb, wb[1])
        y_ref[...] = _dot((_silu(h1) * h3).astype(BF16), wb[2])

    @pl.when(i >= nu_ref[0])
    def _():
        y_ref[...] = jnp.zeros_like(y_ref)


def _experts(x_sorted, blk_exp, n_used, w1, w3, w2, layer):
    n_slots, d = x_sorted.shape
    hid = w1.shape[-1]
    n_blocks = n_slots // MOE_BLOCK
    grid_spec = pltpu.PrefetchScalarGridSpec(
        num_scalar_prefetch=2,
        grid=(n_blocks,),
        in_specs=[pl.BlockSpec((MOE_BLOCK, d), lambda i, be, nu: (i, 0)),
                  pl.BlockSpec((None, None, d, hid), lambda i, be, nu: (layer, be[i], 0, 0)),
                  pl.BlockSpec((None, None, d, hid), lambda i, be, nu: (layer, be[i], 0, 0)),
                  pl.BlockSpec((None, None, hid, d), lambda i, be, nu: (layer, be[i], 0, 0))],
        out_specs=pl.BlockSpec((MOE_BLOCK, d), lambda i, be, nu: (i, 0)),
        scratch_shapes=[pltpu.VMEM((3, d, hid), BF16)],
    )
    assert d == hid
    return pl.pallas_call(
        _expert_kernel,
        grid_spec=grid_spec,
        out_shape=jax.ShapeDtypeStruct((n_slots, d), F32),
        compiler_params=_cparams(("arbitrary",)),
        name="moe_experts",
    )(blk_exp, n_used, x_sorted, w1, w3, w2)


def _combine_kernel(dcur_ref, dnext_ref, x_ref, route_ref, mod_ref, fg_ref, y_ref, o_ref, buf, sem, *,
                    tm, final, n_steps):
    g = pl.program_id(0)
    slot = g % 2

    def row_copy(dest_ref, sl):
        def make(i, k):
            return pltpu.make_async_copy(y_ref.at[pl.ds(dest_ref[0, k * tm + i], 1)],
                                         buf.at[sl, k, pl.ds(i, 1)], sem.at[sl])
        return make

    @pl.when(g == 0)
    def _():
        _issue_rows(row_copy(dcur_ref, 0), tm)

    @pl.when(g + 1 < n_steps)
    def _():
        _issue_rows(row_copy(dnext_ref, 1 - slot), tm)

    _drain_rows(row_copy(dcur_ref, slot), tm)

    r = route_ref[...]
    f = buf[slot, 0] * r[:, 2:3] + buf[slot, 1] * r[:, 3:4]
    x = x_ref[...] + mod_ref[5:6, :] * f
    if final:
        x = x * lax.rsqrt(jnp.mean(x * x, axis=-1, keepdims=True) + NORM_EPS) * fg_ref[...]
    o_ref[...] = x


def _combine(xs, route, mod, final_g, y_sorted, dest, n_ctx, final):
    b_, t_, d = xs.shape
    tm = 256
    nt = t_ // tm
    n_ct = n_ctx // tm
    skip = n_ct if final else 0
    ntl = nt - skip
    n_steps = b_ * ntl

    def bt(g):
        return g // ntl, g % ntl + skip

    def tok(w):
        return pl.BlockSpec((None, tm, w), lambda g: (*bt(g), 0))

    def dspec(off):
        def imap(g):
            b, t = bt(jnp.minimum(g + off, n_steps - 1))
            return b * nt + t, 0, 0
        return pl.BlockSpec((None, 1, 2 * tm), imap, memory_space=pltpu.SMEM)

    return pl.pallas_call(
        functools.partial(_combine_kernel, tm=tm, final=final, n_steps=n_steps),
        grid=(n_steps,),
        in_specs=[dspec(0), dspec(1), tok(d), tok(LANES),
                  pl.BlockSpec((None, None, 6, d), lambda g: (g // ntl, jnp.where(g % ntl + skip < n_ct, 0, 1), 0, 0)),
                  pl.BlockSpec((1, d), lambda g: (0, 0)),
                  pl.BlockSpec(memory_space=pl.ANY)],
        out_specs=pl.BlockSpec((None, tm, d), lambda g: (g // ntl, g % ntl, 0)),
        out_shape=jax.ShapeDtypeStruct((b_, t_ - skip * tm, d), F32),
        scratch_shapes=[pltpu.VMEM((2, 2, tm, d), F32), pltpu.SemaphoreType.DMA((2,))],
        compiler_params=_cparams(("arbitrary",)),
        name="moe_combine",
    )(dest, dest, xs, route, mod, final_g.reshape(1, d), y_sorted)


def _slot_plan(route, counts, tm):
    b_, t_, _ = route.shape
    n_tok = b_ * t_
    cnt = counts[0, EXPERT_LANE0:EXPERT_LANE0 + MOE_EXPERTS].astype(jnp.int32)
    padded = (cnt + MOE_BLOCK - 1) // MOE_BLOCK * MOE_BLOCK
    pad_end = jnp.cumsum(padded)
    pad_start = pad_end - padded
    n_blocks = (n_tok * 2 + MOE_EXPERTS * (MOE_BLOCK - 1) + MOE_BLOCK - 1) // MOE_BLOCK
    n_used = (pad_end[-1] // MOE_BLOCK).astype(jnp.int32)
    blk = jnp.minimum(jnp.arange(n_blocks, dtype=jnp.int32), n_used - 1) * MOE_BLOCK
    blk_exp = jnp.sum((pad_end[None, :] <= blk[:, None]).astype(jnp.int32), axis=1)
    blk_exp = jnp.minimum(blk_exp, MOE_EXPERTS - 1).astype(jnp.int32)
    e = route[..., 0:2].astype(jnp.int32)
    rank = route[..., 4:6].astype(jnp.int32)
    eid = jnp.arange(MOE_EXPERTS, dtype=jnp.int32)
    dest = rank + jnp.sum(jnp.where(e[..., None] == eid, pad_start, 0), axis=-1)
    dest = dest.reshape(b_ * t_ // tm, tm, 2).transpose(0, 2, 1).reshape(b_ * t_ // tm, 1, 2 * tm)
    plan = jnp.concatenate([pad_end, padded, n_used.reshape(1)]).astype(jnp.int32)
    return dest, blk_exp, n_used.reshape(1), plan, n_blocks * MOE_BLOCK


def _permute_w_in(w_in):
    parts = [w_in[:, _REF_OFF[name][0]:_REF_OFF[name][0] + _REF_OFF[name][1]] for name in _OUR_ORDER]
    width = sum(part.shape[1] for part in parts)
    parts.append(jnp.zeros((w_in.shape[0], N_P - width), w_in.dtype))
    return jnp.concatenate(parts, axis=1).astype(BF16)


def _rope_tables(n_ctx, l_):
    half = SCAN_DK // 2
    inv = ROPE_BASE ** (-jnp.arange(half, dtype=F32) / half)
    ang = jnp.arange(n_ctx + l_, dtype=F32)[:, None] * inv[None, :]
    ret_cos = jnp.concatenate([jnp.cos(ang), jnp.cos(ang)], axis=-1)
    ret_sin = jnp.concatenate([-jnp.sin(ang), jnp.sin(ang)], axis=-1)
    q = ATT_HEAD_DIM // 4
    inv = ROPE_BASE ** (-jnp.arange(q, dtype=F32) / q)
    tpos = jnp.arange(l_)
    a_r = (tpos // GRID_W).astype(F32)[:, None] * inv[None, :]
    a_c = (tpos % GRID_W).astype(F32)[:, None] * inv[None, :]
    att_cos = jnp.concatenate([jnp.cos(a_r), jnp.cos(a_r), jnp.cos(a_c), jnp.cos(a_c)], axis=-1)
    att_sin = jnp.concatenate([-jnp.sin(a_r), jnp.sin(a_r), -jnp.sin(a_c), jnp.sin(a_c)], axis=-1)
    q_scale = ATT_HEAD_DIM ** -0.5
    return ret_cos, ret_sin, att_cos, att_sin, att_cos * q_scale, att_sin * q_scale


def kernel(x, c, ctx, c_ctx, w_ada, b_ada, norm1_g, norm2_g, w_in, gla_wa2, gla_ba, gla_norm_g, attn_sink,
           ret_norm_g, w_br_gla, w_br_attn, w_br_ret, w_out, moe_w_grp, moe_b_grp, moe_w_exp, moe_b_exp,
           moe_w1, moe_w3, moe_w2, final_g):
    b_, l_, d = x.shape
    n_ctx = ctx.shape[1]
    depth = w_ada.shape[0]
    assert d == D_MODEL and n_ctx % 256 == 0 and l_ % 256 == 0 and b_ <= 8

    xs = jnp.concatenate([ctx, x], axis=1)
    cc = jnp.zeros((16, d), F32).at[:b_].set(c).at[b_].set(c_ctx)
    mod_all = _modulation(cc, w_ada, b_ada)
    ret_cos, ret_sin, att_cos, att_sin, att_cos_q, att_sin_q = _rope_tables(n_ctx, l_)

    for layer in range(depth):
        last = layer == depth - 1
        m = mod_all[layer].reshape(16, 6, d)
        mod = jnp.stack([jnp.broadcast_to(m[b_], (b_, 6, d)), m[:b_]], axis=1)

        w_p = _permute_w_in(w_in[layer])
        wa2_pad = jnp.zeros((2, LANES, 512), F32)
        wa2_pad = wa2_pad.at[0, 0:GLA_LOW_RANK].set(gla_wa2[layer, 0])
        wa2_pad = wa2_pad.at[1, GLA_LOW_RANK:2 * GLA_LOW_RANK].set(gla_wa2[layer, 1]).astype(BF16)
        ba = gla_ba[layer].reshape(2, 1, 512)

        p = _in_projection(xs, mod, norm1_g[layer], w_p, n_ctx)
        obg, obr = _scan(p, ret_cos, ret_sin, wa2_pad, ba, n_ctx, True)
        yg, yr = _scan(p, ret_cos, ret_sin, wa2_pad, ba, n_ctx, False,
                       (obg, obr, gla_norm_g[layer], ret_norm_g[layer]))
        ya = _attention(p, attn_sink[layer], att_cos, att_sin, att_cos_q, att_sin_q, n_ctx)

        w_route = jnp.zeros((d, LANES), F32)
        w_route = w_route.at[:, 0:MOE_GROUPS].set(moe_w_grp[layer])
        w_route = w_route.at[:, EXPERT_LANE0:EXPERT_LANE0 + MOE_EXPERTS].set(moe_w_exp[layer])
        wr_hi = w_route.astype(BF16)
        wr_lo = (w_route - wr_hi.astype(F32)).astype(BF16)
        b_route = jnp.zeros((1, LANES), F32)
        b_route = b_route.at[0, 0:MOE_GROUPS].set(moe_b_grp[layer])
        b_route = b_route.at[0, EXPERT_LANE0:EXPERT_LANE0 + MOE_EXPERTS].set(moe_b_exp[layer])

        xs, h2, route, counts = _merge(
            yg, ya, yr, p, xs, mod, norm2_g[layer],
            w_br_gla[layer].astype(BF16), w_br_attn[layer].astype(BF16), w_br_ret[layer].astype(BF16),
            w_out[layer].astype(BF16), wr_hi, jnp.concatenate([wr_hi, wr_lo], axis=1), b_route, n_ctx)

        dest, blk_exp, n_used, plan, n_slots = _slot_plan(route, counts, 256)
        x_sorted = _dispatch(h2, dest, plan, n_slots)
        y_sorted = _experts(x_sorted, blk_exp, n_used, moe_w1, moe_w3, moe_w2, layer)
        xs = _combine(xs, route, mod, final_g, y_sorted, dest, n_ctx, last)

    return xs
```
